```python
import math
import jax, jax.numpy as jnp
from jax import lax
import numpy as np

D_MODEL = 2048
BATCH = 16
SEQ = 2048
DEPTH = 1
DEC_BATCH = 4
DEC_SEQ = 4096
PAST_LEN = 128

DILATED_GROUPS = ((128, 1), (512, 4), (2048, 16))
N_DIL_GROUPS = 3
GROUP_HEADS = 4
ATTN_HEAD_DIM = 128
ATTN_HEADS = N_DIL_GROUPS * GROUP_HEADS
ATTN_WIDTH = ATTN_HEADS * ATTN_HEAD_DIM
ATTN_OUT = GROUP_HEADS * ATTN_HEAD_DIM
FNET_GROUPS = 4
FNET_GROUP_DIM = 256
FNET_WIDTH = FNET_GROUPS * FNET_GROUP_DIM
MEM_TOKENS = 256
MEM_HEADS = 4
MEM_HEAD_DIM = 256
MEM_WIDTH = MEM_HEADS * MEM_HEAD_DIM
IN_WIDTH = 3 * ATTN_WIDTH + FNET_WIDTH + MEM_WIDTH
N_BRANCHES = 3
D_FF = 5632
N_BUCKETS = 32
MAX_EXACT = 8
REL_MAX_DIST = 1024
RMS_EPS = 1e-6
NEG_INF = -1e30

kernel_name = "hybrid_dilated_fnet_memory_encoder"


def _rms(x, g):
    xf = x.astype(jnp.float32)
    y = xf * lax.rsqrt(jnp.mean(xf * xf, axis=-1, keepdims=True) + RMS_EPS)
    return (y * g.astype(jnp.float32)).astype(x.dtype)


def _swiglu(h, w_in, w_out):
    a, b = jnp.split(h @ w_in, 2, axis=-1)
    return (jax.nn.silu(a) * b) @ w_out


def _t5_bucket(rel):
    half = N_BUCKETS // 2
    n = jnp.abs(rel)
    nf = jnp.maximum(n, 1).astype(jnp.float32)
    large = MAX_EXACT + (jnp.log(nf / MAX_EXACT) / math.log(REL_MAX_DIST / MAX_EXACT)
                         * (half - MAX_EXACT)).astype(jnp.int32)
    large = jnp.minimum(large, half - 1)
    return jnp.where(rel > 0, half, 0) + jnp.where(n < MAX_EXACT, n, large)


def _dilated_group(q, k, v, window, dil, bias_table):
    B, S, H, E = q.shape
    L = S // dil
    R = window // (2 * dil)
    nblk = -(-L // R)
    Lp = nblk * R

    def sub(t):
        return t.reshape(B, L, dil, H, E).transpose(0, 2, 1, 3, 4)

    qs = jnp.pad(sub(q), ((0, 0), (0, 0), (0, Lp - L), (0, 0), (0, 0))).reshape(B, dil, nblk, R, H, E)

    def key_blocks(t):
        tp = jnp.pad(sub(t), ((0, 0), (0, 0), (R, Lp - L + R), (0, 0), (0, 0)))
        tp = tp.reshape(B, dil, nblk + 2, R, H, E)
        return jnp.concatenate([tp[:, :, :-2], tp[:, :, 1:-1], tp[:, :, 2:]], axis=3)

    kb = key_blocks(k)
    vb = key_blocks(v)
    s_idx = jnp.arange(R)[:, None]
    t_idx = jnp.arange(3 * R)[None, :]
    delta = t_idx - R - s_idx
    bias = bias_table[_t5_bucket(delta * dil)].transpose(2, 0, 1).astype(jnp.float32)
    key_sub = jnp.arange(nblk)[:, None, None] * R - R + t_idx[None]
    valid = (jnp.abs(delta) <= R)[None] & (key_sub >= 0) & (key_sub < L)

    logits = jnp.einsum('bdnqhe,bdnkhe->bdnhqk', qs, kb).astype(jnp.float32) * (E ** -0.5) + bias
    logits = jnp.where(valid[:, None], logits, NEG_INF)
    m = jnp.max(logits, axis=-1, keepdims=True)
    p = jnp.exp(logits - m)
    den = jnp.sum(p, axis=-1, keepdims=True)
    o = jnp.einsum('bdnhqk,bdnkhe->bdnqhe', p / den, vb.astype(jnp.float32))
    lse = (m + jnp.log(den))[..., 0]
    o = o.reshape(B, dil, Lp, H, E)[:, :, :L].transpose(0, 2, 1, 3, 4).reshape(B, S, H, E)
    lse = lse.transpose(0, 1, 2, 4, 3).reshape(B, dil, Lp, H)[:, :, :L].transpose(0, 2, 1, 3).reshape(B, S, H)
    return o, lse


def _dilated_mixture(q, k, v, rel_bias):
    B, S = q.shape[:2]
    outs, lses = [], []
    for g, (window, dil) in enumerate(DILATED_GROUPS):
        o, lse = _dilated_group(q[:, :, g], k[:, :, g], v[:, :, g], window, dil,
                                rel_bias[:, g * GROUP_HEADS:(g + 1) * GROUP_HEADS])
        outs.append(o)
        lses.append(lse)
    alpha = jax.nn.softmax(jnp.stack(lses, axis=0), axis=0)
    o = jnp.sum(alpha[..., None] * jnp.stack(outs, axis=0), axis=0)
    return o.reshape(B, S, ATTN_OUT).astype(q.dtype)


def _fourier(u):
    B, S, _ = u.shape
    ug = u.reshape(B, S, FNET_GROUPS, FNET_GROUP_DIM).astype(jnp.float32)
    f = jnp.fft.fft2(ug, axes=(1, 3), norm="ortho").real
    return f.reshape(B, S, FNET_WIDTH).astype(u.dtype)


def _memory(q_m, mem, mem_norm, w_mem_kv):
    B, S, _ = q_m.shape
    M = mem.shape[1]
    kv = (_rms(mem, mem_norm) @ w_mem_kv).reshape(B, M, 2, MEM_HEADS, MEM_HEAD_DIM)
    q = q_m.reshape(B, S, MEM_HEADS, MEM_HEAD_DIM)
    logits = jnp.einsum('bshe,bmhe->bhsm', q, kv[:, :, 0]).astype(jnp.float32) * (MEM_HEAD_DIM ** -0.5)
    p = jax.nn.softmax(logits, axis=-1)
    o = jnp.einsum('bhsm,bmhe->bshe', p, kv[:, :, 1].astype(jnp.float32))
    return o.reshape(B, S, MEM_WIDTH).astype(q_m.dtype)


def _layer(x, mem, rel_bias, f1_pre, f1_wi, f1_wo, f1_post, mix_pre, mem_norm, w_in, w_mem_kv,
           w_gate, b_gate, w_pa, w_pf, w_pm, w_out, mix_post, f2_pre, f2_wi, f2_wo, f2_post):
    B, S, D = x.shape
    x = x + 0.5 * _rms(_swiglu(_rms(x, f1_pre), f1_wi, f1_wo), f1_post)
    h = _rms(x, mix_pre)
    z = h @ w_in
    qkv_a = z[..., :3 * ATTN_WIDTH].reshape(B, S, 3, N_DIL_GROUPS, GROUP_HEADS, ATTN_HEAD_DIM)
    u_f = z[..., 3 * ATTN_WIDTH:3 * ATTN_WIDTH + FNET_WIDTH]
    q_m = z[..., 3 * ATTN_WIDTH + FNET_WIDTH:]
    a = _dilated_mixture(qkv_a[:, :, 0], qkv_a[:, :, 1], qkv_a[:, :, 2], rel_bias)
    f = _fourier(u_f)
    m = _memory(q_m, mem, mem_norm, w_mem_kv)
    gates = jax.nn.sigmoid((h @ w_gate + b_gate).astype(jnp.float32)).astype(x.dtype).reshape(B, S, N_BRANCHES, D)
    merged = gates[:, :, 0] * (a @ w_pa) + gates[:, :, 1] * (f @ w_pf) + gates[:, :, 2] * (m @ w_pm)
    x = x + _rms(merged @ w_out, mix_post)
    x = x + 0.5 * _rms(_swiglu(_rms(x, f2_pre), f2_wi, f2_wo), f2_post)
    return x


def setup_inputs(seed: int = 0) -> dict:
    key = jax.random.key(seed)
    ks = jax.random.split(key, 32)
    f32 = jnp.float32

    def w(k, shape, fan_in):
        return jax.random.normal(k, shape, f32) * (fan_in ** -0.5)

    def gain(k):
        return 1.0 + 0.01 * jax.random.normal(k, (DEPTH, D_MODEL), f32)

    return {
        "x_prompt": jax.random.normal(ks[0], (BATCH, SEQ, D_MODEL), f32),
        "x_sample": jax.random.normal(ks[1], (DEC_BATCH, DEC_SEQ, D_MODEL), f32),
        "mem_prompt": jax.random.normal(ks[2], (BATCH, MEM_TOKENS, D_MODEL), f32),
        "mem_sample": jax.random.normal(ks[3], (DEC_BATCH, MEM_TOKENS, D_MODEL), f32),
        "rel_bias": 0.1 * jax.random.normal(ks[4], (N_BUCKETS, ATTN_HEADS), f32),
        "ffn1_norm_pre": gain(ks[5]),
        "ffn1_w_in": w(ks[6], (DEPTH, D_MODEL, 2 * D_FF), D_MODEL),
        "ffn1_w_out": w(ks[7], (DEPTH, D_FF, D_MODEL), D_FF),
        "ffn1_norm_post": gain(ks[8]),
        "mix_norm_pre": gain(ks[9]),
        "mem_norm": gain(ks[10]),
        "w_in": w(ks[11], (DEPTH, D_MODEL, IN_WIDTH), D_MODEL),
        "w_mem_kv": w(ks[12], (DEPTH, D_MODEL, 2 * MEM_WIDTH), D_MODEL),
        "w_gate": w(ks[13], (DEPTH, D_MODEL, N_BRANCHES * D_MODEL), D_MODEL),
        "b_gate": 0.01 * jax.random.normal(ks[14], (DEPTH, N_BRANCHES * D_MODEL), f32),
        "w_proj_attn": w(ks[15], (DEPTH, ATTN_OUT, D_MODEL), ATTN_OUT),
        "w_proj_fnet": w(ks[16], (DEPTH, FNET_WIDTH, D_MODEL), FNET_WIDTH),
        "w_proj_mem": w(ks[17], (DEPTH, MEM_WIDTH, D_MODEL), MEM_WIDTH),
        "w_out": w(ks[18], (DEPTH, D_MODEL, D_MODEL), D_MODEL),
        "mix_norm_post": gain(ks[19]),
        "ffn2_norm_pre": gain(ks[20]),
        "ffn2_w_in": w(ks[21], (DEPTH, D_MODEL, 2 * D_FF), D_MODEL),
        "ffn2_w_out": w(ks[22], (DEPTH, D_FF, D_MODEL), D_FF),
        "ffn2_norm_post": gain(ks[23]),
    }


def reference(x_prompt, x_sample, mem_prompt, mem_sample, rel_bias,
              ffn1_norm_pre, ffn1_w_in, ffn1_w_out, ffn1_norm_post,
              mix_norm_pre, mem_norm, w_in, w_mem_kv, w_gate, b_gate,
              w_proj_attn, w_proj_fnet, w_proj_mem, w_out, mix_norm_post,
              ffn2_norm_pre, ffn2_w_in, ffn2_w_out, ffn2_norm_post):
    y_prompt = x_prompt
    y_sample = x_sample
    for l in range(DEPTH):
        lw = (ffn1_norm_pre[l], ffn1_w_in[l], ffn1_w_out[l], ffn1_norm_post[l],
              mix_norm_pre[l], mem_norm[l], w_in[l], w_mem_kv[l], w_gate[l], b_gate[l],
              w_proj_attn[l], w_proj_fnet[l], w_proj_mem[l], w_out[l], mix_norm_post[l],
              ffn2_norm_pre[l], ffn2_w_in[l], ffn2_w_out[l], ffn2_norm_post[l])
        y_prompt = _layer(y_prompt, mem_prompt, rel_bias, *lw)
        y_sample = _layer(y_sample, mem_sample, rel_bias, *lw)
    return (y_prompt, y_sample)
```

```python
import functools
import math

import jax
import jax.numpy as jnp
from jax import lax
from jax.experimental import pallas as pl
from jax.experimental.pallas import tpu as pltpu

F32 = jnp.float32
BF16 = jnp.bfloat16

RMS_EPS = 1e-6
NEG_INF = -1e30

DILATED_GROUPS = ((128, 1), (512, 4), (2048, 16))
GROUP_HEADS = 4
HEAD_DIM = 128
GROUP_WIDTH = GROUP_HEADS * HEAD_DIM
ATTN_WIDTH = len(DILATED_GROUPS) * GROUP_WIDTH
FNET_GROUPS = 4
FNET_GROUP_DIM = 256
FNET_WIDTH = FNET_GROUPS * FNET_GROUP_DIM
MEM_HEADS = 4
MEM_HEAD_DIM = 256
MEM_WIDTH = MEM_HEADS * MEM_HEAD_DIM
N_BUCKETS = 32
MAX_EXACT = 8
REL_MAX_DIST = 1024

Q_BLK = 128
HALF_WIN = 64
KEY_WIN = 256
COL_BLK = 512

V7X_VMEM_LIMIT_BYTES = 58 * 1024 * 1024


def _cparams(semantics, vmem_bytes=V7X_VMEM_LIMIT_BYTES):
    return pltpu.CompilerParams(dimension_semantics=semantics, vmem_limit_bytes=vmem_bytes)


def _rms_f32(x, g):
    return x * lax.rsqrt(jnp.mean(x * x, axis=-1, keepdims=True) + RMS_EPS) * g


def _ffn_body(emit_next, x_ref, gpre_ref, wia_ref, wib_ref, wo_ref, gpost_ref, gnext_ref,
              out_ref, *rest):
    if emit_next:
        hn_ref, h_scr = rest
    else:
        (h_scr,) = rest
    j = pl.program_id(1)
    nj = pl.num_programs(1)

    @pl.when(j == 0)
    def _():
        h_scr[...] = _rms_f32(x_ref[...], gpre_ref[...]).astype(BF16)

    h = h_scr[...]
    a = jnp.dot(h, wia_ref[...], preferred_element_type=F32)
    b = jnp.dot(h, wib_ref[...], preferred_element_type=F32)
    g = (a * jax.nn.sigmoid(a) * b).astype(BF16)
    c = jnp.dot(g, wo_ref[...], preferred_element_type=F32)

    @pl.when(j == 0)
    def _():
        out_ref[...] = c

    @pl.when(j > 0)
    def _():
        out_ref[...] += c

    @pl.when(j == nj - 1)
    def _():
        o = x_ref[...] + 0.5 * _rms_f32(out_ref[...], gpost_ref[...])
        out_ref[...] = o
        if emit_next:
            hn_ref[...] = _rms_f32(o, gnext_ref[...]).astype(BF16)


def _ffn(x, g_pre, w_in, w_out, g_post, g_next, *, tm=512, tf=512):
    T, D = x.shape
    F = w_out.shape[0]
    nj = F // tf
    emit_next = g_next is not None
    gn = g_next if emit_next else g_post
    row = lambda i, j: (i, 0)
    const = lambda i, j: (0, 0)
    out_shape = [jax.ShapeDtypeStruct((T, D), F32)]
    out_specs = [pl.BlockSpec((tm, D), row)]
    if emit_next:
        out_shape.append(jax.ShapeDtypeStruct((T, D), BF16))
        out_specs.append(pl.BlockSpec((tm, D), row))
    res = pl.pallas_call(
        functools.partial(_ffn_body, emit_next),
        out_shape=out_shape,
        grid=(T // tm, nj),
        in_specs=[
            pl.BlockSpec((tm, D), row),
            pl.BlockSpec((1, D), const),
            pl.BlockSpec((D, tf), lambda i, j: (0, j)),
            pl.BlockSpec((D, tf), lambda i, j: (0, j + nj)),
            pl.BlockSpec((tf, D), lambda i, j: (j, 0)),
            pl.BlockSpec((1, D), const),
            pl.BlockSpec((1, D), const),
        ],
        out_specs=out_specs,
        scratch_shapes=[pltpu.VMEM((tm, D), BF16)],
        compiler_params=_cparams(("parallel", "arbitrary")),
        name="ffn",
    )(x, g_pre, w_in, w_in, w_out, g_post, gn)
    return (res[0], res[1]) if emit_next else (res[0], None)


def _matmul_body(h_ref, w_ref, o_ref):
    o_ref[...] = jnp.dot(h_ref[...], w_ref[...], preferred_element_type=F32).astype(o_ref.dtype)


def _in_proj(h, w, *, tm=512, tn=3328):
    T, D = h.shape
    N = w.shape[1]
    return pl.pallas_call(
        _matmul_body,
        out_shape=jax.ShapeDtypeStruct((T, N), BF16),
        grid=(N // tn, T // tm),
        in_specs=[pl.BlockSpec((tm, D), lambda n, i: (i, 0)),
                  pl.BlockSpec((D, tn), lambda n, i: (0, n))],
        out_specs=pl.BlockSpec((tm, tn), lambda n, i: (i, n)),
        compiler_params=_cparams(("parallel", "parallel")),
        name="in_proj",
    )(h, w)


def _t5_bucket(rel):
    half = N_BUCKETS // 2
    n = jnp.abs(rel)
    nf = jnp.maximum(n, 1).astype(F32)
    large = MAX_EXACT + (jnp.log(nf / MAX_EXACT) / math.log(REL_MAX_DIST / MAX_EXACT)
                         * (half - MAX_EXACT)).astype(jnp.int32)
    large = jnp.minimum(large, half - 1)
    return jnp.where(rel > 0, half, 0) + jnp.where(n < MAX_EXACT, n, large)


WIN_OFFSETS = (-HALF_WIN, 0, Q_BLK - KEY_WIN)


def _bias_body(tbl_ref, out_ref):
    g = pl.program_id(0)
    w = pl.program_id(1)
    dil = jnp.where(g == 0, DILATED_GROUPS[0][1],
                    jnp.where(g == 1, DILATED_GROUPS[1][1], DILATED_GROUPS[2][1]))
    off = jnp.where(w == 0, WIN_OFFSETS[0], jnp.where(w == 1, WIN_OFFSETS[1], WIN_OFFSETS[2]))
    s = lax.broadcasted_iota(jnp.int32, (Q_BLK, KEY_WIN), 0)
    t = lax.broadcasted_iota(jnp.int32, (Q_BLK, KEY_WIN), 1)
    delta = t - s + off
    bucket = _t5_bucket(delta * dil)
    valid = jnp.abs(delta) <= HALF_WIN
    for h in range(GROUP_HEADS):
        col = g * GROUP_HEADS + h

        def pick(i, acc):
            return jnp.where(bucket == i, tbl_ref[i, col], acc)

        bias = lax.fori_loop(0, N_BUCKETS, pick, jnp.zeros((Q_BLK, KEY_WIN), F32))
        out_ref[0, 0, h] = jnp.where(valid, bias, NEG_INF)


def _bias_tables(rel_bias):
    ng = len(DILATED_GROUPS)
    return pl.pallas_call(
        _bias_body,
        out_shape=jax.ShapeDtypeStruct((ng, len(WIN_OFFSETS), GROUP_HEADS, Q_BLK, KEY_WIN), F32),
        grid=(ng, len(WIN_OFFSETS)),
        in_specs=[pl.BlockSpec(memory_space=pltpu.SMEM)],
        out_specs=pl.BlockSpec((1, 1, GROUP_HEADS, Q_BLK, KEY_WIN), lambda g, w: (g, w, 0, 0, 0)),
        compiler_params=_cparams(("arbitrary", "arbitrary")),
        name="rel_bias",
    )(rel_bias)


def _attn_body(L, W, TL, q_ref, k_ref, v_ref, bias_ref, o_ref, lse_ref):
    scale = HEAD_DIM ** -0.5
    nblk = L // Q_BLK
    n0 = pl.program_id(2) * (TL // Q_BLK)

    def block(i, carry):
        n = n0 + i
        qs = pl.multiple_of(i * Q_BLK, Q_BLK)
        ks = pl.multiple_of(jnp.clip(n * Q_BLK - HALF_WIN, 0, L - W), HALF_WIN)
        kind = jnp.where(n == 0, 1, jnp.where(n == nblk - 1, 2, 0))
        for h in range(GROUP_HEADS):
            cols = slice(h * HEAD_DIM, (h + 1) * HEAD_DIM)
            qh = q_ref[0, pl.ds(qs, Q_BLK), cols]
            kh = k_ref[0, pl.ds(ks, W), cols]
            vh = v_ref[0, pl.ds(ks, W), cols]
            s = lax.dot_general(qh, kh, (((1,), (1,)), ((), ())), preferred_element_type=F32)
            s = s * scale + bias_ref[0, kind, h, :, :W]
            m = jnp.max(s, axis=-1, keepdims=True)
            p = jnp.exp(s - m)
            den = jnp.sum(p, axis=-1, keepdims=True)
            o = jnp.dot(p.astype(BF16), vh, preferred_element_type=F32) / den
            o_ref[0, pl.ds(qs, Q_BLK), cols] = o
            lse_ref[0, pl.ds(qs, Q_BLK), cols] = jnp.broadcast_to(m + jnp.log(den), (Q_BLK, HEAD_DIM))
        return carry

    lax.fori_loop(0, TL // Q_BLK, block, 0)


def _dilated_group(z3, bias, g, dil, *, tl_max=1024):
    B, S, C = z3.shape
    L = S // dil
    W = min(KEY_WIN, L)
    TL = min(tl_max, L)
    ncb = C // COL_BLK
    zv = z3.reshape(B, L, dil * C)
    kq = ATTN_WIDTH // COL_BLK
    kv_blk = (1, L, GROUP_WIDTH)
    q_blk = (1, TL, GROUP_WIDTH)
    out_sds = jax.ShapeDtypeStruct((B, L, dil * GROUP_WIDTH), F32)
    o, lse = pl.pallas_call(
        functools.partial(_attn_body, L, W, TL),
        out_shape=[out_sds, out_sds],
        grid=(B, dil, L // TL),
        in_specs=[
            pl.BlockSpec(q_blk, lambda b, r, c: (b, c, r * ncb + g)),
            pl.BlockSpec(kv_blk, lambda b, r, c: (b, 0, r * ncb + kq + g)),
            pl.BlockSpec(kv_blk, lambda b, r, c: (b, 0, r * ncb + 2 * kq + g)),
            pl.BlockSpec((1, len(WIN_OFFSETS), GROUP_HEADS, Q_BLK, W), lambda b, r, c: (g, 0, 0, 0, 0)),
        ],
        out_specs=[pl.BlockSpec(q_blk, lambda b, r, c: (b, c, r)),
                   pl.BlockSpec(q_blk, lambda b, r, c: (b, c, r))],
        compiler_params=_cparams(("parallel", "parallel", "arbitrary")),
        name=f"dilated_attn_g{g}",
    )(zv, zv, zv, bias)
    return o.reshape(B * S, GROUP_WIDTH), lse.reshape(B * S, GROUP_WIDTH)


def _dft_table_body(n, neg_sin, o_ref):
    rows = o_ref.shape[0]
    r = lax.broadcasted_iota(jnp.int32, (rows, n), 0) + pl.program_id(0) * rows
    c = lax.broadcasted_iota(jnp.int32, (rows, n), 1)
    ang = ((r * c) & (n - 1)).astype(F32) * (2.0 * math.pi / n)
    sn = jnp.sin(ang)
    o_ref[:, :n] = jnp.cos(ang).astype(o_ref.dtype)
    o_ref[:, n:] = (-sn if neg_sin else sn).astype(o_ref.dtype)


def _dft_table(n, neg_sin, rows):
    assert n & (n - 1) == 0
    return pl.pallas_call(
        functools.partial(_dft_table_body, n, neg_sin),
        out_shape=jax.ShapeDtypeStruct((n, 2 * n), BF16),
        grid=(n // rows,),
        out_specs=pl.BlockSpec((rows, 2 * n), lambda i: (i, 0)),
        compiler_params=_cparams(("parallel",)),
        name=f"dft_table_{n}",
    )()


def _fourier_body(S, rows1, ua_ref, ub_ref, wc_ref, ws_ref, o_ref, y_scr):
    C = FNET_GROUP_DIM

    @pl.when(pl.program_id(1) == 0)
    def _():
        for u_ref, base in ((ua_ref, 0), (ub_ref, COL_BLK)):
            for k in range(COL_BLK // C):
                for r0 in range(0, S, rows1):
                    y = jnp.dot(u_ref[0, r0:r0 + rows1, k * C:(k + 1) * C], wc_ref[...],
                                preferred_element_type=F32)
                    c0 = base + k * C
                    y_scr[r0:r0 + rows1, c0:c0 + C] = y[:, :C].astype(BF16)
                    y_scr[S + r0:S + r0 + rows1, c0:c0 + C] = y[:, C:].astype(BF16)

    scale = 1.0 / math.sqrt(S * C)
    f = jnp.dot(ws_ref[...], y_scr[...], preferred_element_type=F32) * scale
    o_ref[0] = f.astype(o_ref.dtype)


def _fourier(z3, wc, ws, *, tr=256, rows1=512):
    B, S, C = z3.shape
    cb = (3 * ATTN_WIDTH) // COL_BLK
    return pl.pallas_call(
        functools.partial(_fourier_body, S, rows1),
        out_shape=jax.ShapeDtypeStruct((B, S, FNET_WIDTH), BF16),
        grid=(B, S // tr),
        in_specs=[
            pl.BlockSpec((1, S, COL_BLK), lambda b, i: (b, 0, cb)),
            pl.BlockSpec((1, S, COL_BLK), lambda b, i: (b, 0, cb + 1)),
            pl.BlockSpec((FNET_GROUP_DIM, 2 * FNET_GROUP_DIM), lambda b, i: (0, 0)),
            pl.BlockSpec((tr, 2 * S), lambda b, i: (i, 0)),
        ],
        out_specs=pl.BlockSpec((1, tr, FNET_WIDTH), lambda b, i: (b, i, 0)),
        scratch_shapes=[pltpu.VMEM((2 * S, FNET_WIDTH), BF16)],
        compiler_params=_cparams(("parallel", "arbitrary")),
        name="fourier",
    )(z3, z3, wc, ws).reshape(B * S, FNET_WIDTH)


def _memory_body(qa_ref, qb_ref, mem_ref, gm_ref, wkv_ref, o_ref, k_scr, v_scr):
    E = MEM_HEAD_DIM

    @pl.when(pl.program_id(1) == 0)
    def _():
        hm = _rms_f32(mem_ref[0], gm_ref[...]).astype(BF16)
        kv = jnp.dot(hm, wkv_ref[...], preferred_element_type=F32)
        k_scr[...] = (kv[:, :MEM_WIDTH] * (E ** -0.5)).astype(BF16)
        v_scr[...] = kv[:, MEM_WIDTH:].astype(BF16)

    for q_ref, base in ((qa_ref, 0), (qb_ref, COL_BLK)):
        for k in range(COL_BLK // E):
            c0 = base + k * E
            s = lax.dot_general(q_ref[0, :, k * E:(k + 1) * E], k_scr[:, c0:c0 + E],
                                (((1,), (1,)), ((), ())), preferred_element_type=F32)
            m = jnp.max(s, axis=-1, keepdims=True)
            p = jnp.exp(s - m)
            den = jnp.sum(p, axis=-1, keepdims=True)
            o = jnp.dot(p.astype(BF16), v_scr[:, c0:c0 + E], preferred_element_type=F32) / den
            o_ref[0, :, c0:c0 + E] = o.astype(o_ref.dtype)


def _memory(z3, mem, g_mem, w_kv, *, tq=1024):
    B, S, C = z3.shape
    M, D = mem.shape[1:]
    cb = (3 * ATTN_WIDTH + FNET_WIDTH) // COL_BLK
    assert MEM_HEAD_DIM ** -0.5 == 2.0 ** round(math.log2(MEM_HEAD_DIM ** -0.5))
    return pl.pallas_call(
        _memory_body,
        out_shape=jax.ShapeDtypeStruct((B, S, MEM_WIDTH), BF16),
        grid=(B, S // tq),
        in_specs=[
            pl.BlockSpec((1, tq, COL_BLK), lambda b, i: (b, i, cb)),
            pl.BlockSpec((1, tq, COL_BLK), lambda b, i: (b, i, cb + 1)),
            pl.BlockSpec((1, M, D), lambda b, i: (b, 0, 0)),
            pl.BlockSpec((1, D), lambda b, i: (0, 0)),
            pl.BlockSpec((D, 2 * MEM_WIDTH), lambda b, i: (0, 0)),
        ],
        out_specs=pl.BlockSpec((1, tq, MEM_WIDTH), lambda b, i: (b, i, 0)),
        scratch_shapes=[pltpu.VMEM((M, MEM_WIDTH), BF16), pltpu.VMEM((M, MEM_WIDTH), BF16)],
        compiler_params=_cparams(("parallel", "arbitrary")),
        name="memory_attn",
    )(z3, z3, mem, g_mem, w_kv).reshape(B * S, MEM_WIDTH)


def _merge_body(x_ref, h_ref, o0_ref, o1_ref, o2_ref, l0_ref, l1_ref, l2_ref, f_ref, m_ref,
                wg0_ref, wg1_ref, wg2_ref, bg0_ref, bg1_ref, bg2_ref, wpa_ref, wpf_ref, wpm_ref,
                wo_ref, gpost_ref, out_ref, a_scr):
    j = pl.program_id(1)
    nj = pl.num_programs(1)

    @pl.when(j == 0)
    def _():
        l0, l1, l2 = l0_ref[...], l1_ref[...], l2_ref[...]
        mx = jnp.maximum(jnp.maximum(l0, l1), l2)
        e0, e1, e2 = jnp.exp(l0 - mx), jnp.exp(l1 - mx), jnp.exp(l2 - mx)
        a = (e0 * o0_ref[...] + e1 * o1_ref[...] + e2 * o2_ref[...]) / (e0 + e1 + e2)
        a_scr[...] = a.astype(BF16)

    h = h_ref[...]

    def gate(w_ref, b_ref):
        return jax.nn.sigmoid(jnp.dot(h, w_ref[...], preferred_element_type=F32) + b_ref[...])

    merged = gate(wg0_ref, bg0_ref) * jnp.dot(a_scr[...], wpa_ref[...], preferred_element_type=F32)
    merged += gate(wg1_ref, bg1_ref) * jnp.dot(f_ref[...], wpf_ref[...], preferred_element_type=F32)
    merged += gate(wg2_ref, bg2_ref) * jnp.dot(m_ref[...], wpm_ref[...], preferred_element_type=F32)
    c = jnp.dot(merged.astype(BF16), wo_ref[...], preferred_element_type=F32)

    @pl.when(j == 0)
    def _():
        out_ref[...] = c

    @pl.when(j > 0)
    def _():
        out_ref[...] += c

    @pl.when(j == nj - 1)
    def _():
        out_ref[...] = x_ref[...] + _rms_f32(out_ref[...], gpost_ref[...])


def _merge(x, h, o_groups, lse_groups, f, m, w_gate, b_gate, w_pa, w_pf, w_pm, w_out, g_post,
           *, tm=512, tn=256):
    T, D = x.shape
    nj = D // tn
    row = lambda i, j: (i, 0)
    const = lambda i, j: (0, 0)
    colj = lambda i, j: (0, j)
    gate_specs = [pl.BlockSpec((D, tn), lambda i, j, k=k: (0, j + k * nj)) for k in range(3)]
    bias_specs = [pl.BlockSpec((1, tn), lambda i, j, k=k: (0, j + k * nj)) for k in range(3)]
    return pl.pallas_call(
        _merge_body,
        out_shape=jax.ShapeDtypeStruct((T, D), F32),
        grid=(T // tm, nj),
        in_specs=[pl.BlockSpec((tm, D), row), pl.BlockSpec((tm, D), row)]
                 + [pl.BlockSpec((tm, GROUP_WIDTH), row)] * 6
                 + [pl.BlockSpec((tm, FNET_WIDTH), row), pl.BlockSpec((tm, MEM_WIDTH), row)]
                 + gate_specs + bias_specs
                 + [pl.BlockSpec((GROUP_WIDTH, tn), colj), pl.BlockSpec((FNET_WIDTH, tn), colj),
                    pl.BlockSpec((MEM_WIDTH, tn), colj), pl.BlockSpec((tn, D), lambda i, j: (j, 0)),
                    pl.BlockSpec((1, D), const)],
        out_specs=pl.BlockSpec((tm, D), row),
        scratch_shapes=[pltpu.VMEM((tm, GROUP_WIDTH), BF16)],
        compiler_params=_cparams(("parallel", "arbitrary")),
        name="merge",
    )(x, h, *o_groups, *lse_groups, f, m, w_gate, w_gate, w_gate, b_gate, b_gate, b_gate,
      w_pa, w_pf, w_pm, w_out, g_post)


def _layer(x, mem, bias, wc, ws, p):
    B, S, D = x.shape
    xt = x.reshape(B * S, D)
    x1, h = _ffn(xt, p["f1_pre"], p["f1_wi"], p["f1_wo"], p["f1_post"], p["mix_pre"])
    z3 = _in_proj(h, p["w_in"]).reshape(B, S, -1)
    o_groups, lse_groups = [], []
    for g, (_, dil) in enumerate(DILATED_GROUPS):
        o, lse = _dilated_group(z3, bias, g, dil)
        o_groups.append(o)
        lse_groups.append(lse)
    f = _fourier(z3, wc, ws)
    m = _memory(z3, mem, p["mem_norm"], p["w_mem_kv"])
    x2 = _merge(x1, h, o_groups, lse_groups, f, m, p["w_gate"], p["b_gate"], p["w_pa"], p["w_pf"],
                p["w_pm"], p["w_out"], p["mix_post"])
    x3, _ = _ffn(x2, p["f2_pre"], p["f2_wi"], p["f2_wo"], p["f2_post"], None)
    return x3.reshape(B, S, D)


def kernel(x_prompt, x_sample, mem_prompt, mem_sample, rel_bias, ffn1_norm_pre, ffn1_w_in, ffn1_w_out, ffn1_norm_post, mix_norm_pre, mem_norm, w_in, w_mem_kv, w_gate, b_gate, w_proj_attn, w_proj_fnet, w_proj_mem, w_out, mix_norm_post, ffn2_norm_pre, ffn2_w_in, ffn2_w_out, ffn2_norm_post):
    depth = w_in.shape[0]
    bias = _bias_tables(rel_bias)
    wc = _dft_table(FNET_GROUP_DIM, False, FNET_GROUP_DIM)
    ws = {s: _dft_table(s, True, 256) for s in {x_prompt.shape[1], x_sample.shape[1]}}
    y_prompt, y_sample = x_prompt, x_sample
    for l in range(depth):
        vec = lambda a: a[l][None, :]
        mat = lambda a: a[l].astype(BF16)
        p = dict(f1_pre=vec(ffn1_norm_pre), f1_wi=mat(ffn1_w_in), f1_wo=mat(ffn1_w_out),
                 f1_post=vec(ffn1_norm_post), mix_pre=vec(mix_norm_pre), mem_norm=vec(mem_norm),
                 w_in=mat(w_in), w_mem_kv=mat(w_mem_kv), w_gate=mat(w_gate), b_gate=vec(b_gate),
                 w_pa=mat(w_proj_attn), w_pf=mat(w_proj_fnet), w_pm=mat(w_proj_mem),
                 w_out=mat(w_out), mix_post=vec(mix_norm_post), f2_pre=vec(ffn2_norm_pre),
                 f2_wi=mat(ffn2_w_in), f2_wo=mat(ffn2_w_out), f2_post=vec(ffn2_norm_post))
        y_prompt = _layer(y_prompt, mem_prompt, bias, wc, ws[y_prompt.shape[1]], p)
        y_sample = _layer(y_sample, mem_sample, bias, wc, ws[y_sample.shape[1]], p)
    return (y_prompt, y_sample)
```

```python
import functools
import math

import jax
import jax.numpy as jnp
from jax import lax
from jax.experimental import pallas as pl
from jax.experimental.pallas import tpu as pltpu

F32 = jnp.float32
BF16 = jnp.bfloat16

RMS_EPS = 1e-6
NEG_INF = -1e30

DILATED_GROUPS = ((128, 1), (512, 4), (2048, 16))
GROUP_HEADS = 4
HEAD_DIM = 128
GROUP_WIDTH = GROUP_HEADS * HEAD_DIM
ATTN_WIDTH = len(DILATED_GROUPS) * GROUP_WIDTH
FNET_GROUPS = 4
FNET_GROUP_DIM = 256
FNET_WIDTH = FNET_GROUPS * FNET_GROUP_DIM
MEM_HEADS = 4
MEM_HEAD_DIM = 256
MEM_WIDTH = MEM_HEADS * MEM_HEAD_DIM
N_BUCKETS = 32
MAX_EXACT = 8
REL_MAX_DIST = 1024

Q_BLK = 128
HALF_WIN = 64
KEY_WIN = 256
COL_BLK = 512
NAT_FNET_COL = 3 * GROUP_WIDTH
NAT_MEM_COL = NAT_FNET_COL + FNET_WIDTH

V7X_VMEM_LIMIT_BYTES = 58 * 1024 * 1024
LANES = 128


def _cparams(semantics, vmem_bytes=V7X_VMEM_LIMIT_BYTES):
    return pltpu.CompilerParams(dimension_semantics=semantics, vmem_limit_bytes=vmem_bytes)


def _rms_f32(x, g):
    return x * lax.rsqrt(jnp.mean(x * x, axis=-1, keepdims=True) + RMS_EPS) * g


def _ffn_body(next_dils, x_ref, gpre_ref, wia_ref, wib_ref, wo_ref, gpost_ref, gnext_ref,
              out_ref, *rest):
    hn_refs = rest[:len(next_dils)]
    scratch = rest[len(next_dils):]
    h_scr = scratch[0]
    tm = x_ref.shape[0]
    j = pl.program_id(1)
    nj = pl.num_programs(1)

    @pl.when(j == 0)
    def _():
        h_scr[...] = _rms_f32(x_ref[...], gpre_ref[...]).astype(BF16)
        out_ref[...] = jnp.zeros_like(out_ref)

    h = h_scr[...]
    a = jnp.dot(h, wia_ref[...], preferred_element_type=F32)
    b = jnp.dot(h, wib_ref[...], preferred_element_type=F32)
    g = (a * jax.nn.sigmoid(a) * b).astype(BF16)
    out_ref[...] += jnp.dot(g, wo_ref[...], preferred_element_type=F32)

    @pl.when(j == nj - 1)
    def _():
        o = x_ref[...] + 0.5 * _rms_f32(out_ref[...], gpost_ref[...])
        out_ref[...] = o
        if next_dils:
            hn_scr = scratch[1]
            hn = _rms_f32(o, gnext_ref[...])
            for c in range(hn_scr.shape[0]):
                hn_scr[c] = hn[:, c * LANES:(c + 1) * LANES]
            for hn_ref, dil in zip(hn_refs, next_dils):
                if dil == 1:
                    hn_ref[0, 0] = hn.astype(BF16)
                    continue
                n = tm // dil
                for r in range(dil):
                    for c in range(hn_scr.shape[0]):
                        hn_ref[0, r, :, c * LANES:(c + 1) * LANES] = (
                            hn_scr[c, pl.ds(r, n, stride=dil), :].astype(BF16))


def _ffn(x, seq, g_pre, w_in, w_out, g_post, g_next, next_dils, *, tm=512, tf=512):
    T, D = x.shape
    F = w_out.shape[0]
    nj = F // tf
    tps = seq // tm
    row = lambda i, j: (i, 0)
    const = lambda i, j: (0, 0)
    out_shape = [jax.ShapeDtypeStruct((T, D), F32)]
    out_specs = [pl.BlockSpec((tm, D), row)]
    scratch = [pltpu.VMEM((tm, D), BF16)]
    for dil in next_dils:
        out_shape.append(jax.ShapeDtypeStruct((T // seq, dil, seq // dil, D), BF16))
        out_specs.append(pl.BlockSpec((1, dil, tm // dil, D), lambda i, j: (i // tps, 0, i % tps, 0)))
    if next_dils:
        scratch.append(pltpu.VMEM((D // LANES, tm, LANES), F32))
    res = pl.pallas_call(
        functools.partial(_ffn_body, tuple(next_dils)),
        out_shape=out_shape,
        grid=(T // tm, nj),
        in_specs=[
            pl.BlockSpec((tm, D), row),
            pl.BlockSpec((1, D), const),
            pl.BlockSpec((D, tf), lambda i, j: (0, j)),
            pl.BlockSpec((D, tf), lambda i, j: (0, j + nj)),
            pl.BlockSpec((tf, D), lambda i, j: (j, 0)),
            pl.BlockSpec((1, D), const),
            pl.BlockSpec((1, D), const),
        ],
        out_specs=out_specs,
        scratch_shapes=scratch,
        compiler_params=_cparams(("parallel", "arbitrary")),
        name="ffn",
    )(x, g_pre, w_in, w_in, w_out, g_post, g_next)
    return res[0], res[1:]


def _matmul_body(h_ref, w_ref, o_ref):
    o_ref[...] = jnp.dot(h_ref[...], w_ref[...], preferred_element_type=F32).astype(o_ref.dtype)


def _in_proj(h, w, *, tm=1024, tn_max=1792):
    T, D = h.shape
    N = w.shape[1]
    tn = min(N, tn_max)
    return pl.pallas_call(
        _matmul_body,
        out_shape=jax.ShapeDtypeStruct((T, N), BF16),
        grid=(N // tn, T // tm),
        in_specs=[pl.BlockSpec((tm, D), lambda n, i: (i, 0)),
                  pl.BlockSpec((D, tn), lambda n, i: (0, n))],
        out_specs=pl.BlockSpec((tm, tn), lambda n, i: (i, n)),
        compiler_params=_cparams(("parallel", "parallel")),
        name="in_proj",
    )(h, w)


def _t5_bucket(rel):
    half = N_BUCKETS // 2
    n = jnp.abs(rel)
    nf = jnp.maximum(n, 1).astype(F32)
    large = MAX_EXACT + (jnp.log(nf / MAX_EXACT) / math.log(REL_MAX_DIST / MAX_EXACT)
                         * (half - MAX_EXACT)).astype(jnp.int32)
    large = jnp.minimum(large, half - 1)
    return jnp.where(rel > 0, half, 0) + jnp.where(n < MAX_EXACT, n, large)


WIN_OFFSETS = (-HALF_WIN, 0, Q_BLK - KEY_WIN)


def _bias_body(tbl_ref, out_ref):
    g = pl.program_id(0)
    w = pl.program_id(1)
    dil = jnp.where(g == 0, DILATED_GROUPS[0][1],
                    jnp.where(g == 1, DILATED_GROUPS[1][1], DILATED_GROUPS[2][1]))
    off = jnp.where(w == 0, WIN_OFFSETS[0], jnp.where(w == 1, WIN_OFFSETS[1], WIN_OFFSETS[2]))
    s = lax.broadcasted_iota(jnp.int32, (Q_BLK, KEY_WIN), 0)
    t = lax.broadcasted_iota(jnp.int32, (Q_BLK, KEY_WIN), 1)
    delta = t - s + off
    bucket = _t5_bucket(delta * dil)
    valid = jnp.abs(delta) <= HALF_WIN
    for h in range(GROUP_HEADS):
        col = g * GROUP_HEADS + h

        def pick(i, acc):
            return jnp.where(bucket == i, tbl_ref[i, col], acc)

        bias = lax.fori_loop(0, N_BUCKETS, pick, jnp.zeros((Q_BLK, KEY_WIN), F32))
        out_ref[0, 0, h] = jnp.where(valid, bias, NEG_INF)


def _bias_tables(rel_bias):
    ng = len(DILATED_GROUPS)
    return pl.pallas_call(
        _bias_body,
        out_shape=jax.ShapeDtypeStruct((ng, len(WIN_OFFSETS), GROUP_HEADS, Q_BLK, KEY_WIN), F32),
        grid=(ng, len(WIN_OFFSETS)),
        in_specs=[pl.BlockSpec(memory_space=pltpu.SMEM)],
        out_specs=pl.BlockSpec((1, 1, GROUP_HEADS, Q_BLK, KEY_WIN), lambda g, w: (g, w, 0, 0, 0)),
        compiler_params=_cparams(("arbitrary", "arbitrary")),
        name="rel_bias",
    )(rel_bias)


def _attn_body(L, W, TL, q_ref, k_ref, v_ref, bias_ref, o_ref, lse_ref):
    scale = HEAD_DIM ** -0.5
    nblk = L // Q_BLK
    n0 = pl.program_id(2) * (TL // Q_BLK)

    def block(i, carry):
        n = n0 + i
        qs = pl.multiple_of(i * Q_BLK, Q_BLK)
        ks = pl.multiple_of(jnp.clip(n * Q_BLK - HALF_WIN, 0, L - W), HALF_WIN)
        kind = jnp.where(n == 0, 1, jnp.where(n == nblk - 1, 2, 0))
        for h in range(GROUP_HEADS):
            cols = slice(h * HEAD_DIM, (h + 1) * HEAD_DIM)
            qh = q_ref[0, 0, pl.ds(qs, Q_BLK), cols]
            kh = k_ref[0, 0, pl.ds(ks, W), cols]
            vh = v_ref[0, 0, pl.ds(ks, W), cols]
            s = lax.dot_general(qh, kh, (((1,), (1,)), ((), ())), preferred_element_type=F32)
            s = s * scale + bias_ref[0, kind, h, :, :W]
            m = jnp.max(s, axis=-1, keepdims=True)
            p = jnp.exp(s - m)
            den = jnp.sum(p, axis=-1, keepdims=True)
            o = jnp.dot(p.astype(BF16), vh, preferred_element_type=F32) / den
            o_ref[0, 0, pl.ds(qs, Q_BLK), cols] = o
            lse_ref[0, 0, pl.ds(qs, Q_BLK), cols] = jnp.broadcast_to(m + jnp.log(den), (Q_BLK, HEAD_DIM))
        return carry

    lax.fori_loop(0, TL // Q_BLK, block, 0)


def _dilated_group(zg, bias, g, *, tl_max=1024):
    B, dil, L, _ = zg.shape
    W = min(KEY_WIN, L)
    TL = min(tl_max, L)
    kv_blk = (1, 1, L, GROUP_WIDTH)
    q_blk = (1, 1, TL, GROUP_WIDTH)
    out_sds = jax.ShapeDtypeStruct((B, dil, L, GROUP_WIDTH), F32)
    return pl.pallas_call(
        functools.partial(_attn_body, L, W, TL),
        out_shape=[out_sds, out_sds],
        grid=(B, dil, L // TL),
        in_specs=[
            pl.BlockSpec(q_blk, lambda b, r, c: (b, r, c, 0)),
            pl.BlockSpec(kv_blk, lambda b, r, c: (b, r, 0, 1)),
            pl.BlockSpec(kv_blk, lambda b, r, c: (b, r, 0, 2)),
            pl.BlockSpec((1, len(WIN_OFFSETS), GROUP_HEADS, Q_BLK, W), lambda b, r, c: (g, 0, 0, 0, 0)),
        ],
        out_specs=[pl.BlockSpec(q_blk, lambda b, r, c: (b, r, c, 0)),
                   pl.BlockSpec(q_blk, lambda b, r, c: (b, r, c, 0))],
        compiler_params=_cparams(("parallel", "parallel", "arbitrary")),
        name=f"dilated_attn_g{g}",
    )(zg, zg, zg, bias)


def _dft_table_body(n, neg_sin, o_ref):
    rows = o_ref.shape[0]
    r = lax.broadcasted_iota(jnp.int32, (rows, n), 0) + pl.program_id(0) * rows
    c = lax.broadcasted_iota(jnp.int32, (rows, n), 1)
    ang = ((r * c) & (n - 1)).astype(F32) * (2.0 * math.pi / n)
    sn = jnp.sin(ang)
    o_ref[:, :n] = jnp.cos(ang).astype(o_ref.dtype)
    o_ref[:, n:] = (-sn if neg_sin else sn).astype(o_ref.dtype)


def _dft_table(n, neg_sin, rows):
    assert n & (n - 1) == 0
    return pl.pallas_call(
        functools.partial(_dft_table_body, n, neg_sin),
        out_shape=jax.ShapeDtypeStruct((n, 2 * n), BF16),
        grid=(n // rows,),
        out_specs=pl.BlockSpec((rows, 2 * n), lambda i: (i, 0)),
        compiler_params=_cparams(("parallel",)),
        name=f"dft_table_{n}",
    )()


def _fourier_body(S, rows1, ua_ref, ub_ref, wc_ref, ws_ref, o_ref, y_scr):
    C = FNET_GROUP_DIM

    @pl.when(pl.program_id(1) == 0)
    def _():
        for u_ref, base in ((ua_ref, 0), (ub_ref, COL_BLK)):
            for k in range(COL_BLK // C):
                for r0 in range(0, S, rows1):
                    y = jnp.dot(u_ref[0, r0:r0 + rows1, k * C:(k + 1) * C], wc_ref[...],
                                preferred_element_type=F32)
                    c0 = base + k * C
                    y_scr[r0:r0 + rows1, c0:c0 + C] = y[:, :C].astype(BF16)
                    y_scr[S + r0:S + r0 + rows1, c0:c0 + C] = y[:, C:].astype(BF16)

    scale = 1.0 / math.sqrt(S * C)
    f = jnp.dot(ws_ref[...], y_scr[...], preferred_element_type=F32) * scale
    o_ref[0] = f.astype(o_ref.dtype)


def _fourier(z3, wc, ws, *, tr=256, rows1=512):
    B, S, C = z3.shape
    cb = NAT_FNET_COL // COL_BLK
    return pl.pallas_call(
        functools.partial(_fourier_body, S, rows1),
        out_shape=jax.ShapeDtypeStruct((B, S, FNET_WIDTH), BF16),
        grid=(B, S // tr),
        in_specs=[
            pl.BlockSpec((1, S, COL_BLK), lambda b, i: (b, 0, cb)),
            pl.BlockSpec((1, S, COL_BLK), lambda b, i: (b, 0, cb + 1)),
            pl.BlockSpec((FNET_GROUP_DIM, 2 * FNET_GROUP_DIM), lambda b, i: (0, 0)),
            pl.BlockSpec((tr, 2 * S), lambda b, i: (i, 0)),
        ],
        out_specs=pl.BlockSpec((1, tr, FNET_WIDTH), lambda b, i: (b, i, 0)),
        scratch_shapes=[pltpu.VMEM((2 * S, FNET_WIDTH), BF16)],
        compiler_params=_cparams(("parallel", "arbitrary")),
        name="fourier",
    )(z3, z3, wc, ws).reshape(B * S, FNET_WIDTH)


def _memory_body(qa_ref, qb_ref, mem_ref, gm_ref, wkv_ref, o_ref, k_scr, v_scr):
    E = MEM_HEAD_DIM

    @pl.when(pl.program_id(1) == 0)
    def _():
        hm = _rms_f32(mem_ref[0], gm_ref[...]).astype(BF16)
        kv = jnp.dot(hm, wkv_ref[...], preferred_element_type=F32)
        k_scr[...] = (kv[:, :MEM_WIDTH] * (E ** -0.5)).astype(BF16)
        v_scr[...] = kv[:, MEM_WIDTH:].astype(BF16)

    for q_ref, base in ((qa_ref, 0), (qb_ref, COL_BLK)):
        for k in range(COL_BLK // E):
            c0 = base + k * E
            s = lax.dot_general(q_ref[0, :, k * E:(k + 1) * E], k_scr[:, c0:c0 + E],
                                (((1,), (1,)), ((), ())), preferred_element_type=F32)
            m = jnp.max(s, axis=-1, keepdims=True)
            p = jnp.exp(s - m)
            den = jnp.sum(p, axis=-1, keepdims=True)
            o = jnp.dot(p.astype(BF16), v_scr[:, c0:c0 + E], preferred_element_type=F32) / den
            o_ref[0, :, c0:c0 + E] = o.astype(o_ref.dtype)


def _memory(z3, mem, g_mem, w_kv, *, tq=1024):
    B, S, C = z3.shape
    M, D = mem.shape[1:]
    cb = NAT_MEM_COL // COL_BLK
    assert MEM_HEAD_DIM ** -0.5 == 2.0 ** round(math.log2(MEM_HEAD_DIM ** -0.5))
    return pl.pallas_call(
        _memory_body,
        out_shape=jax.ShapeDtypeStruct((B, S, MEM_WIDTH), BF16),
        grid=(B, S // tq),
        in_specs=[
            pl.BlockSpec((1, tq, COL_BLK), lambda b, i: (b, i, cb)),
            pl.BlockSpec((1, tq, COL_BLK), lambda b, i: (b, i, cb + 1)),
            pl.BlockSpec((1, M, D), lambda b, i: (b, 0, 0)),
            pl.BlockSpec((1, D), lambda b, i: (0, 0)),
            pl.BlockSpec((D, 2 * MEM_WIDTH), lambda b, i: (0, 0)),
        ],
        out_specs=pl.BlockSpec((1, tq, MEM_WIDTH), lambda b, i: (b, i, 0)),
        scratch_shapes=[pltpu.VMEM((M, MEM_WIDTH), BF16), pltpu.VMEM((M, MEM_WIDTH), BF16)],
        compiler_params=_cparams(("parallel", "arbitrary")),
        name="memory_attn",
    )(z3, z3, mem, g_mem, w_kv).reshape(B * S, MEM_WIDTH)


def _merge_body(x_ref, h_ref, o0_ref, o1_ref, o2_ref, l0_ref, l1_ref, l2_ref, f_ref, m_ref,
                wg0_ref, wg1_ref, wg2_ref, bg0_ref, bg1_ref, bg2_ref, wpa_ref, wpf_ref, wpm_ref,
                wo_ref, gpost_ref, out_ref, a_scr, nat_scr):
    j = pl.program_id(1)
    nj = pl.num_programs(1)
    tm = x_ref.shape[0]

    @pl.when(j == 0)
    def _():
        def natural(ref, slot):
            dil = ref.shape[1]
            if dil == 1:
                return ref[0, 0]
            nc = GROUP_WIDTH // LANES
            for r in range(dil):
                for c in range(nc):
                    nat_scr[slot * nc + c, pl.ds(r, tm // dil, stride=dil), :] = (
                        ref[0, r, :, c * LANES:(c + 1) * LANES])
            return jnp.concatenate([nat_scr[slot * nc + c] for c in range(nc)], axis=1)

        l0, l1, l2 = natural(l0_ref, None), natural(l1_ref, 0), natural(l2_ref, 1)
        mx = jnp.maximum(jnp.maximum(l0, l1), l2)
        e0, e1, e2 = jnp.exp(l0 - mx), jnp.exp(l1 - mx), jnp.exp(l2 - mx)
        num = e0 * natural(o0_ref, None) + e1 * natural(o1_ref, 2) + e2 * natural(o2_ref, 3)
        a_scr[...] = (num / (e0 + e1 + e2)).astype(BF16)
        out_ref[...] = jnp.zeros_like(out_ref)

    h = h_ref[...]

    def gate(w_ref, b_ref):
        return jax.nn.sigmoid(jnp.dot(h, w_ref[...], preferred_element_type=F32) + b_ref[...])

    merged = gate(wg0_ref, bg0_ref) * jnp.dot(a_scr[...], wpa_ref[...], preferred_element_type=F32)
    merged += gate(wg1_ref, bg1_ref) * jnp.dot(f_ref[...], wpf_ref[...], preferred_element_type=F32)
    merged += gate(wg2_ref, bg2_ref) * jnp.dot(m_ref[...], wpm_ref[...], preferred_element_type=F32)
    out_ref[...] += jnp.dot(merged.astype(BF16), wo_ref[...], preferred_element_type=F32)

    @pl.when(j == nj - 1)
    def _():
        out_ref[...] = x_ref[...] + _rms_f32(out_ref[...], gpost_ref[...])


def _merge(x, seq, h, o_groups, lse_groups, f, m, w_gate, b_gate, w_pa, w_pf, w_pm, w_out, g_post,
           *, tm=512, tn=256):
    T, D = x.shape
    nj = D // tn
    tps = seq // tm
    row = lambda i, j: (i, 0)
    const = lambda i, j: (0, 0)
    colj = lambda i, j: (0, j)
    group_specs = [pl.BlockSpec((1, a.shape[1], tm // a.shape[1], GROUP_WIDTH),
                                lambda i, j: (i // tps, 0, i % tps, 0))
                   for a in (*o_groups, *lse_groups)]
    gate_specs = [pl.BlockSpec((D, tn), lambda i, j, k=k: (0, j + k * nj)) for k in range(3)]
    bias_specs = [pl.BlockSpec((1, tn), lambda i, j, k=k: (0, j + k * nj)) for k in range(3)]
    return pl.pallas_call(
        _merge_body,
        out_shape=jax.ShapeDtypeStruct((T, D), F32),
        grid=(T // tm, nj),
        in_specs=[pl.BlockSpec((tm, D), row), pl.BlockSpec((tm, D), row)]
                 + group_specs
                 + [pl.BlockSpec((tm, FNET_WIDTH), row), pl.BlockSpec((tm, MEM_WIDTH), row)]
                 + gate_specs + bias_specs
                 + [pl.BlockSpec((GROUP_WIDTH, tn), colj), pl.BlockSpec((FNET_WIDTH, tn), colj),
                    pl.BlockSpec((MEM_WIDTH, tn), colj), pl.BlockSpec((tn, D), lambda i, j: (j, 0)),
                    pl.BlockSpec((1, D), const)],
        out_specs=pl.BlockSpec((tm, D), row),
        scratch_shapes=[pltpu.VMEM((tm, GROUP_WIDTH), BF16),
                        pltpu.VMEM((2 * (len(DILATED_GROUPS) - 1) * GROUP_WIDTH // LANES, tm, LANES), F32)],
        compiler_params=_cparams(("parallel", "arbitrary")),
        name="merge",
    )(x, h, *o_groups, *lse_groups, f, m, w_gate, w_gate, w_gate, b_gate, b_gate, b_gate,
      w_pa, w_pf, w_pm, w_out, g_post)


def _layer_params(l, ffn1_norm_pre, ffn1_w_in, ffn1_w_out, ffn1_norm_post, mix_norm_pre, mem_norm, w_in,
                  w_mem_kv, w_gate, b_gate, w_proj_attn, w_proj_fnet, w_proj_mem, w_out, mix_norm_post,
                  ffn2_norm_pre, ffn2_w_in, ffn2_w_out, ffn2_norm_post):
    vec = lambda a: a[l][None, :]
    mat = lambda a: a[l].astype(BF16)
    wi = mat(w_in)
    qkv = lambda g: [wi[:, s * ATTN_WIDTH + g * GROUP_WIDTH: s * ATTN_WIDTH + (g + 1) * GROUP_WIDTH]
                     for s in range(3)]
    w_nat = jnp.concatenate(qkv(0) + [wi[:, 3 * ATTN_WIDTH:]], axis=1)
    w_groups = [jnp.concatenate(qkv(g), axis=1) for g in range(1, len(DILATED_GROUPS))]
    return dict(f1_pre=vec(ffn1_norm_pre), f1_wi=mat(ffn1_w_in), f1_wo=mat(ffn1_w_out),
                f1_post=vec(ffn1_norm_post), mix_pre=vec(mix_norm_pre), mem_norm=vec(mem_norm),
                w_nat=w_nat, w_groups=w_groups, w_mem_kv=mat(w_mem_kv), w_gate=mat(w_gate),
                b_gate=vec(b_gate), w_pa=mat(w_proj_attn), w_pf=mat(w_proj_fnet), w_pm=mat(w_proj_mem),
                w_out=mat(w_out), mix_post=vec(mix_norm_post), f2_pre=vec(ffn2_norm_pre),
                f2_wi=mat(ffn2_w_in), f2_wo=mat(ffn2_w_out), f2_post=vec(ffn2_norm_post))


def _layer(x, mem, bias, wc, ws, p):
    B, S, D = x.shape
    dils = [dil for _, dil in DILATED_GROUPS]
    assert dils[0] == 1
    xt = x.reshape(B * S, D)
    x1, hs = _ffn(xt, S, p["f1_pre"], p["f1_wi"], p["f1_wo"], p["f1_post"], p["mix_pre"], dils)
    h = hs[0].reshape(B * S, D)
    z0 = _in_proj(h, p["w_nat"])
    zs = [z0] + [_in_proj(hd.reshape(B * S, D), w) for hd, w in zip(hs[1:], p["w_groups"])]
    o_groups, lse_groups = [], []
    for g, (dil, z) in enumerate(zip(dils, zs)):
        o, lse = _dilated_group(z.reshape(B, dil, S // dil, z.shape[-1]), bias, g)
        o_groups.append(o)
        lse_groups.append(lse)
    z3 = z0.reshape(B, S, -1)
    f = _fourier(z3, wc, ws)
    m = _memory(z3, mem, p["mem_norm"], p["w_mem_kv"])
    x2 = _merge(x1, S, h, o_groups, lse_groups, f, m, p["w_gate"], p["b_gate"], p["w_pa"], p["w_pf"],
                p["w_pm"], p["w_out"], p["mix_post"])
    x3, _ = _ffn(x2, S, p["f2_pre"], p["f2_wi"], p["f2_wo"], p["f2_post"], p["f2_post"], ())
    return x3.reshape(B, S, D)


def kernel(x_prompt, x_sample, mem_prompt, mem_sample, rel_bias, ffn1_norm_pre, ffn1_w_in, ffn1_w_out, ffn1_norm_post, mix_norm_pre, mem_norm, w_in, w_mem_kv, w_gate, b_gate, w_proj_attn, w_proj_fnet, w_proj_mem, w_out, mix_norm_post, ffn2_norm_pre, ffn2_w_in, ffn2_w_out, ffn2_norm_post):
    depth = w_in.shape[0]
    bias = _bias_tables(rel_bias)
    wc = _dft_table(FNET_GROUP_DIM, False, FNET_GROUP_DIM)
    ws = {s: _dft_table(s, True, 256) for s in {x_prompt.shape[1], x_sample.shape[1]}}
    y_prompt, y_sample = x_prompt, x_sample
    for l in range(depth):
        p = _layer_params(l, ffn1_norm_pre, ffn1_w_in, ffn1_w_out, ffn1_norm_post, mix_norm_pre, mem_norm,
                          w_in, w_mem_kv, w_gate, b_gate, w_proj_attn, w_proj_fnet, w_proj_mem, w_out,
                          mix_norm_post, ffn2_norm_pre, ffn2_w_in, ffn2_w_out, ffn2_norm_post)
        y_prompt = _layer(y_prompt, mem_prompt, bias, wc, ws[y_prompt.shape[1]], p)
        y_sample = _layer(y_sample, mem_sample, bias, wc, ws[y_sample.shape[1]], p)
    return (y_prompt, y_sample)
```

```python
import functools
import math

import jax
import jax.numpy as jnp
from jax import lax
from jax.experimental import pallas as pl
from jax.experimental.pallas import tpu as pltpu

F32 = jnp.float32
BF16 = jnp.bfloat16

RMS_EPS = 1e-6
NEG_INF = -1e30

DILATED_GROUPS = ((128, 1), (512, 4), (2048, 16))
GROUP_HEADS = 4
HEAD_DIM = 128
GROUP_WIDTH = GROUP_HEADS * HEAD_DIM
ATTN_WIDTH = len(DILATED_GROUPS) * GROUP_WIDTH
FNET_GROUPS = 4
FNET_GROUP_DIM = 256
FNET_WIDTH = FNET_GROUPS * FNET_GROUP_DIM
MEM_HEADS = 4
MEM_HEAD_DIM = 256
MEM_WIDTH = MEM_HEADS * MEM_HEAD_DIM
N_BUCKETS = 32
MAX_EXACT = 8
REL_MAX_DIST = 1024

Q_BLK = 128
HALF_WIN = 64
KEY_WIN = 256
COL_BLK = 512
NAT_FNET_COL = 3 * GROUP_WIDTH
NAT_MEM_COL = NAT_FNET_COL + FNET_WIDTH

V7X_VMEM_LIMIT_BYTES = 58 * 1024 * 1024
LANES = 128


def _cparams(semantics, vmem_bytes=V7X_VMEM_LIMIT_BYTES):
    return pltpu.CompilerParams(dimension_semantics=semantics, vmem_limit_bytes=vmem_bytes)


def _rms_f32(x, g):
    return x * lax.rsqrt(jnp.mean(x * x, axis=-1, keepdims=True) + RMS_EPS) * g


ROW_CHUNK = 16


def _for_row_chunks(nrows, fn):
    for k in range(nrows // ROW_CHUNK):
        fn(slice(k * ROW_CHUNK, (k + 1) * ROW_CHUNK))


def _ffn_body(next_dils, x_ref, gpre_ref, wia_ref, wib_ref, wo_ref, gpost_ref, gnext_ref,
              out_ref, *rest):
    hn_refs = rest[:len(next_dils)]
    scratch = rest[len(next_dils):]
    h_scr = scratch[0]
    tm = x_ref.shape[0]
    j = pl.program_id(1)
    nj = pl.num_programs(1)

    def swiglu_out():
        h = h_scr[...]
        a = jnp.dot(h, wia_ref[...], preferred_element_type=F32)
        b = jnp.dot(h, wib_ref[...], preferred_element_type=F32)
        g = (a * jax.nn.sigmoid(a) * b).astype(BF16)
        return jnp.dot(g, wo_ref[...], preferred_element_type=F32)

    @pl.when(j == 0)
    def _():
        gpre = gpre_ref[...]

        def pre(rows):
            h_scr[rows, :] = _rms_f32(x_ref[rows, :], gpre).astype(BF16)

        _for_row_chunks(tm, pre)
        out_ref[...] = swiglu_out()

    @pl.when(j > 0)
    def _():
        out_ref[...] += swiglu_out()

    @pl.when(j == nj - 1)
    def _():
        gpost = 0.5 * gpost_ref[...]
        gnext = gnext_ref[...]

        def post(rows):
            o = x_ref[rows, :] + _rms_f32(out_ref[rows, :], gpost)
            out_ref[rows, :] = o
            if next_dils:
                hn = _rms_f32(o, gnext)
                for hn_ref, dil in zip(hn_refs, next_dils):
                    if dil == 1:
                        hn_ref[0, 0, rows, :] = hn.astype(BF16)
                if len(scratch) > 1:
                    for c in range(scratch[1].shape[0]):
                        scratch[1][c, rows, :] = hn[:, c * LANES:(c + 1) * LANES]

        _for_row_chunks(tm, post)
        prev_scr, prev_dil, level = (scratch[1] if len(scratch) > 1 else None), 1, 2
        for hn_ref, dil in zip(hn_refs, next_dils):
            if dil == 1:
                continue
            ratio, n = dil // prev_dil, tm // dil
            cur_scr = scratch[level] if level < len(scratch) else None
            for rp in range(prev_dil):
                for m in range(ratio):
                    r = rp + prev_dil * m
                    for c in range(prev_scr.shape[0]):
                        piece = prev_scr[c, pl.ds(rp * (tm // prev_dil) + m, n, stride=ratio), :]
                        hn_ref[0, r, :, c * LANES:(c + 1) * LANES] = piece.astype(BF16)
                        if cur_scr is not None:
                            cur_scr[c, r * n:(r + 1) * n, :] = piece
            prev_scr, prev_dil, level = cur_scr, dil, level + 1


def _ffn(x, seq, g_pre, w_in, w_out, g_post, g_next, next_dils, *, tm=512, tf=512):
    T, D = x.shape
    F = w_out.shape[0]
    nj = F // tf
    tps = seq // tm
    row = lambda i, j: (i, 0)
    const = lambda i, j: (0, 0)
    out_shape = [jax.ShapeDtypeStruct((T, D), F32)]
    out_specs = [pl.BlockSpec((tm, D), row)]
    scratch = [pltpu.VMEM((tm, D), BF16)]
    for dil in next_dils:
        out_shape.append(jax.ShapeDtypeStruct((T // seq, dil, seq // dil, D), BF16))
        out_specs.append(pl.BlockSpec((1, dil, tm // dil, D), lambda i, j: (i // tps, 0, i % tps, 0)))
    strided = [dil for dil in next_dils if dil > 1]
    assert all(b % a == 0 for a, b in zip([1] + strided, strided))
    scratch += [pltpu.VMEM((D // LANES, tm, LANES), F32)] * len(strided)
    res = pl.pallas_call(
        functools.partial(_ffn_body, tuple(next_dils)),
        out_shape=out_shape,
        grid=(T // tm, nj),
        in_specs=[
            pl.BlockSpec((tm, D), row),
            pl.BlockSpec((1, D), const),
            pl.BlockSpec((D, tf), lambda i, j: (0, j)),
            pl.BlockSpec((D, tf), lambda i, j: (0, j + nj)),
            pl.BlockSpec((tf, D), lambda i, j: (j, 0)),
            pl.BlockSpec((1, D), const),
            pl.BlockSpec((1, D), const),
        ],
        out_specs=out_specs,
        scratch_shapes=scratch,
        compiler_params=_cparams(("parallel", "arbitrary")),
        name="ffn",
    )(x, g_pre, w_in, w_in, w_out, g_post, g_next)
    return res[0], res[1:]


def _matmul_body(h_ref, w_ref, o_ref):
    o_ref[...] = jnp.dot(h_ref[...], w_ref[...], preferred_element_type=F32).astype(o_ref.dtype)


def _in_proj(h, w, *, tm=1024, tn_max=1792):
    T, D = h.shape
    N = w.shape[1]
    tn = min(N, tn_max)
    return pl.pallas_call(
        _matmul_body,
        out_shape=jax.ShapeDtypeStruct((T, N), BF16),
        grid=(N // tn, T // tm),
        in_specs=[pl.BlockSpec((tm, D), lambda n, i: (i, 0)),
                  pl.BlockSpec((D, tn), lambda n, i: (0, n))],
        out_specs=pl.BlockSpec((tm, tn), lambda n, i: (i, n)),
        compiler_params=_cparams(("parallel", "parallel")),
        name="in_proj",
    )(h, w)


def _t5_bucket(rel):
    half = N_BUCKETS // 2
    n = jnp.abs(rel)
    nf = jnp.maximum(n, 1).astype(F32)
    large = MAX_EXACT + (jnp.log(nf / MAX_EXACT) / math.log(REL_MAX_DIST / MAX_EXACT)
                         * (half - MAX_EXACT)).astype(jnp.int32)
    large = jnp.minimum(large, half - 1)
    return jnp.where(rel > 0, half, 0) + jnp.where(n < MAX_EXACT, n, large)


WIN_OFFSETS = (-HALF_WIN, 0, Q_BLK - KEY_WIN)


def _bias_body(tbl_ref, out_ref):
    g = pl.program_id(0)
    w = pl.program_id(1)
    dil = jnp.where(g == 0, DILATED_GROUPS[0][1],
                    jnp.where(g == 1, DILATED_GROUPS[1][1], DILATED_GROUPS[2][1]))
    off = jnp.where(w == 0, WIN_OFFSETS[0], jnp.where(w == 1, WIN_OFFSETS[1], WIN_OFFSETS[2]))
    s = lax.broadcasted_iota(jnp.int32, (Q_BLK, KEY_WIN), 0)
    t = lax.broadcasted_iota(jnp.int32, (Q_BLK, KEY_WIN), 1)
    delta = t - s + off
    bucket = _t5_bucket(delta * dil)
    valid = jnp.abs(delta) <= HALF_WIN
    for h in range(GROUP_HEADS):
        col = g * GROUP_HEADS + h

        def pick(i, acc):
            return jnp.where(bucket == i, tbl_ref[i, col], acc)

        bias = lax.fori_loop(0, N_BUCKETS, pick, jnp.zeros((Q_BLK, KEY_WIN), F32))
        out_ref[0, 0, h] = jnp.where(valid, bias, NEG_INF)


def _bias_tables(rel_bias):
    ng = len(DILATED_GROUPS)
    return pl.pallas_call(
        _bias_body,
        out_shape=jax.ShapeDtypeStruct((ng, len(WIN_OFFSETS), GROUP_HEADS, Q_BLK, KEY_WIN), F32),
        grid=(ng, len(WIN_OFFSETS)),
        in_specs=[pl.BlockSpec(memory_space=pltpu.SMEM)],
        out_specs=pl.BlockSpec((1, 1, GROUP_HEADS, Q_BLK, KEY_WIN), lambda g, w: (g, w, 0, 0, 0)),
        compiler_params=_cparams(("arbitrary", "arbitrary")),
        name="rel_bias",
    )(rel_bias)


def _attn_body(L, W, TL, q_ref, k_ref, v_ref, bias_ref, o_ref, lse_ref):
    scale = HEAD_DIM ** -0.5
    nblk = L // Q_BLK
    n0 = pl.program_id(2) * (TL // Q_BLK)

    def block(i, carry):
        n = n0 + i
        qs = pl.multiple_of(i * Q_BLK, Q_BLK)
        ks = pl.multiple_of(jnp.clip(n * Q_BLK - HALF_WIN, 0, L - W), HALF_WIN)
        kind = jnp.where(n == 0, 1, jnp.where(n == nblk - 1, 2, 0))
        for h in range(GROUP_HEADS):
            cols = slice(h * HEAD_DIM, (h + 1) * HEAD_DIM)
            qh = q_ref[0, 0, pl.ds(qs, Q_BLK), cols]
            kh = k_ref[0, 0, pl.ds(ks, W), cols]
            vh = v_ref[0, 0, pl.ds(ks, W), cols]
            s = lax.dot_general(qh, kh, (((1,), (1,)), ((), ())), preferred_element_type=F32)
            s = s * scale + bias_ref[0, kind, h, :, :W]
            m = jnp.max(s, axis=-1, keepdims=True)
            p = jnp.exp(s - m)
            den = jnp.sum(p, axis=-1, keepdims=True)
            o = jnp.dot(p.astype(BF16), vh, preferred_element_type=F32) / den
            o_ref[0, 0, pl.ds(qs, Q_BLK), cols] = o
            lse_ref[0, 0, pl.ds(qs, Q_BLK), cols] = jnp.broadcast_to(m + jnp.log(den), (Q_BLK, HEAD_DIM))
        return carry

    lax.fori_loop(0, TL // Q_BLK, block, 0)


def _dilated_group(zg, bias, g, *, tl_max=1024):
    B, dil, L, _ = zg.shape
    W = min(KEY_WIN, L)
    TL = min(tl_max, L)
    kv_blk = (1, 1, L, GROUP_WIDTH)
    q_blk = (1, 1, TL, GROUP_WIDTH)
    out_sds = jax.ShapeDtypeStruct((B, dil, L, GROUP_WIDTH), F32)
    return pl.pallas_call(
        functools.partial(_attn_body, L, W, TL),
        out_shape=[out_sds, out_sds],
        grid=(B, dil, L // TL),
        in_specs=[
            pl.BlockSpec(q_blk, lambda b, r, c: (b, r, c, 0)),
            pl.BlockSpec(kv_blk, lambda b, r, c: (b, r, 0, 1)),
            pl.BlockSpec(kv_blk, lambda b, r, c: (b, r, 0, 2)),
            pl.BlockSpec((1, len(WIN_OFFSETS), GROUP_HEADS, Q_BLK, W), lambda b, r, c: (g, 0, 0, 0, 0)),
        ],
        out_specs=[pl.BlockSpec(q_blk, lambda b, r, c: (b, r, c, 0)),
                   pl.BlockSpec(q_blk, lambda b, r, c: (b, r, c, 0))],
        compiler_params=_cparams(("parallel", "parallel", "arbitrary")),
        name=f"dilated_attn_g{g}",
    )(zg, zg, zg, bias)


def _dft_angle(r, c, n):
    return ((r * c) & (n - 1)).astype(F32) * (2.0 * math.pi / n)


def _dft_base_body(n, cos_ref, sin_ref):
    rows = cos_ref.shape[0]
    ang = _dft_angle(lax.broadcasted_iota(jnp.int32, (rows, n), 0),
                     lax.broadcasted_iota(jnp.int32, (rows, n), 1), n)
    cos_ref[...] = jnp.cos(ang)
    sin_ref[...] = jnp.sin(ang)


def _dft_table_body(n, neg_sin, cb_ref, sb_ref, o_ref):
    rows = o_ref.shape[0]
    ang0 = _dft_angle(pl.program_id(0) * rows, lax.broadcasted_iota(jnp.int32, (1, n), 1), n)
    c0, s0 = jnp.cos(ang0), jnp.sin(ang0)
    cb, sb = cb_ref[...], sb_ref[...]
    sn = s0 * cb + c0 * sb
    o_ref[:, :n] = (c0 * cb - s0 * sb).astype(o_ref.dtype)
    o_ref[:, n:] = (-sn if neg_sin else sn).astype(o_ref.dtype)


def _dft_table(n, neg_sin, rows):
    assert n & (n - 1) == 0
    base_sds = jax.ShapeDtypeStruct((rows, n), F32)
    cb, sb = pl.pallas_call(
        functools.partial(_dft_base_body, n),
        out_shape=[base_sds, base_sds],
        compiler_params=_cparams(()),
        name=f"dft_base_{n}",
    )()
    return pl.pallas_call(
        functools.partial(_dft_table_body, n, neg_sin),
        out_shape=jax.ShapeDtypeStruct((n, 2 * n), BF16),
        grid=(n // rows,),
        in_specs=[pl.BlockSpec((rows, n), lambda i: (0, 0))] * 2,
        out_specs=pl.BlockSpec((rows, 2 * n), lambda i: (i, 0)),
        compiler_params=_cparams(("parallel",)),
        name=f"dft_table_{n}",
    )(cb, sb)


def _fourier_body(S, rows1, ua_ref, ub_ref, wc_ref, ws_ref, o_ref, y_scr):
    C = FNET_GROUP_DIM

    @pl.when(pl.program_id(1) == 0)
    def _():
        for u_ref, base in ((ua_ref, 0), (ub_ref, COL_BLK)):
            for k in range(COL_BLK // C):
                for r0 in range(0, S, rows1):
                    y = jnp.dot(u_ref[0, r0:r0 + rows1, k * C:(k + 1) * C], wc_ref[...],
                                preferred_element_type=F32)
                    c0 = base + k * C
                    y_scr[r0:r0 + rows1, c0:c0 + C] = y[:, :C].astype(BF16)
                    y_scr[S + r0:S + r0 + rows1, c0:c0 + C] = y[:, C:].astype(BF16)

    scale = 1.0 / math.sqrt(S * C)
    f = jnp.dot(ws_ref[...], y_scr[...], preferred_element_type=F32) * scale
    o_ref[0] = f.astype(o_ref.dtype)


def _fourier(z3, wc, ws, *, tr=256, rows1=512):
    B, S, C = z3.shape
    cb = NAT_FNET_COL // COL_BLK
    return pl.pallas_call(
        functools.partial(_fourier_body, S, rows1),
        out_shape=jax.ShapeDtypeStruct((B, S, FNET_WIDTH), BF16),
        grid=(B, S // tr),
        in_specs=[
            pl.BlockSpec((1, S, COL_BLK), lambda b, i: (b, 0, cb)),
            pl.BlockSpec((1, S, COL_BLK), lambda b, i: (b, 0, cb + 1)),
            pl.BlockSpec((FNET_GROUP_DIM, 2 * FNET_GROUP_DIM), lambda b, i: (0, 0)),
            pl.BlockSpec((tr, 2 * S), lambda b, i: (i, 0)),
        ],
        out_specs=pl.BlockSpec((1, tr, FNET_WIDTH), lambda b, i: (b, i, 0)),
        scratch_shapes=[pltpu.VMEM((2 * S, FNET_WIDTH), BF16)],
        compiler_params=_cparams(("parallel", "arbitrary")),
        name="fourier",
    )(z3, z3, wc, ws).reshape(B * S, FNET_WIDTH)


def _memory_body(qa_ref, qb_ref, mem_ref, gm_ref, wkv_ref, o_ref, k_scr, v_scr):
    E = MEM_HEAD_DIM

    @pl.when(pl.program_id(1) == 0)
    def _():
        hm = _rms_f32(mem_ref[0], gm_ref[...]).astype(BF16)
        kv = jnp.dot(hm, wkv_ref[...], preferred_element_type=F32)
        k_scr[...] = (kv[:, :MEM_WIDTH] * (E ** -0.5)).astype(BF16)
        v_scr[...] = kv[:, MEM_WIDTH:].astype(BF16)

    for q_ref, base in ((qa_ref, 0), (qb_ref, COL_BLK)):
        for k in range(COL_BLK // E):
            c0 = base + k * E
            s = lax.dot_general(q_ref[0, :, k * E:(k + 1) * E], k_scr[:, c0:c0 + E],
                                (((1,), (1,)), ((), ())), preferred_element_type=F32)
            m = jnp.max(s, axis=-1, keepdims=True)
            p = jnp.exp(s - m)
            den = jnp.sum(p, axis=-1, keepdims=True)
            o = jnp.dot(p.astype(BF16), v_scr[:, c0:c0 + E], preferred_element_type=F32) / den
            o_ref[0, :, c0:c0 + E] = o.astype(o_ref.dtype)


def _memory(z3, mem, g_mem, w_kv, *, tq=1024):
    B, S, C = z3.shape
    M, D = mem.shape[1:]
    cb = NAT_MEM_COL // COL_BLK
    assert MEM_HEAD_DIM ** -0.5 == 2.0 ** round(math.log2(MEM_HEAD_DIM ** -0.5))
    return pl.pallas_call(
        _memory_body,
        out_shape=jax.ShapeDtypeStruct((B, S, MEM_WIDTH), BF16),
        grid=(B, S // tq),
        in_specs=[
            pl.BlockSpec((1, tq, COL_BLK), lambda b, i: (b, i, cb)),
            pl.BlockSpec((1, tq, COL_BLK), lambda b, i: (b, i, cb + 1)),
            pl.BlockSpec((1, M, D), lambda b, i: (b, 0, 0)),
            pl.BlockSpec((1, D), lambda b, i: (0, 0)),
            pl.BlockSpec((D, 2 * MEM_WIDTH), lambda b, i: (0, 0)),
        ],
        out_specs=pl.BlockSpec((1, tq, MEM_WIDTH), lambda b, i: (b, i, 0)),
        scratch_shapes=[pltpu.VMEM((M, MEM_WIDTH), BF16), pltpu.VMEM((M, MEM_WIDTH), BF16)],
        compiler_params=_cparams(("parallel", "arbitrary")),
        name="memory_attn",
    )(z3, z3, mem, g_mem, w_kv).reshape(B * S, MEM_WIDTH)


def _merge_body(x_ref, h_ref, o0_ref, o1_ref, o2_ref, l0_ref, l1_ref, l2_ref, f_ref, m_ref,
                wg0_ref, wg1_ref, wg2_ref, bg0_ref, bg1_ref, bg2_ref, wpa_ref, wpf_ref, wpm_ref,
                wo_ref, gpost_ref, out_ref, a_scr, nat_scr):
    j = pl.program_id(1)
    nj = pl.num_programs(1)
    tm = x_ref.shape[0]

    nc = GROUP_WIDTH // LANES

    def to_natural(ref, slot):
        dil = ref.shape[1]
        for r in range(dil):
            for c in range(nc):
                nat_scr[slot * nc + c, pl.ds(r, tm // dil, stride=dil), :] = (
                    ref[0, r, :, c * LANES:(c + 1) * LANES])

    def natural_rows(ref, slot, rows):
        if ref.shape[1] == 1:
            return ref[0, 0, rows, :]
        return jnp.concatenate([nat_scr[slot * nc + c, rows, :] for c in range(nc)], axis=1)

    def merged_out():
        h = h_ref[...]

        def gate(w_ref, b_ref):
            return jax.nn.sigmoid(jnp.dot(h, w_ref[...], preferred_element_type=F32) + b_ref[...])

        mg = gate(wg0_ref, bg0_ref) * jnp.dot(a_scr[...], wpa_ref[...], preferred_element_type=F32)
        mg += gate(wg1_ref, bg1_ref) * jnp.dot(f_ref[...], wpf_ref[...], preferred_element_type=F32)
        mg += gate(wg2_ref, bg2_ref) * jnp.dot(m_ref[...], wpm_ref[...], preferred_element_type=F32)
        return jnp.dot(mg.astype(BF16), wo_ref[...], preferred_element_type=F32)

    @pl.when(j == 0)
    def _():
        slots = {}
        for ref in (l1_ref, l2_ref, o1_ref, o2_ref):
            slots[id(ref)] = len(slots)
            to_natural(ref, slots[id(ref)])

        def combine(rows):
            l0 = natural_rows(l0_ref, None, rows)
            l1 = natural_rows(l1_ref, slots[id(l1_ref)], rows)
            l2 = natural_rows(l2_ref, slots[id(l2_ref)], rows)
            mx = jnp.maximum(jnp.maximum(l0, l1), l2)
            e0, e1, e2 = jnp.exp(l0 - mx), jnp.exp(l1 - mx), jnp.exp(l2 - mx)
            num = (e0 * natural_rows(o0_ref, None, rows)
                   + e1 * natural_rows(o1_ref, slots[id(o1_ref)], rows)
                   + e2 * natural_rows(o2_ref, slots[id(o2_ref)], rows))
            a_scr[rows, :] = (num / (e0 + e1 + e2)).astype(BF16)

        _for_row_chunks(tm, combine)
        out_ref[...] = merged_out()

    @pl.when(j > 0)
    def _():
        out_ref[...] += merged_out()

    @pl.when(j == nj - 1)
    def _():
        gpost = gpost_ref[...]

        def post(rows):
            out_ref[rows, :] = x_ref[rows, :] + _rms_f32(out_ref[rows, :], gpost)

        _for_row_chunks(tm, post)


def _merge(x, seq, h, o_groups, lse_groups, f, m, w_gate, b_gate, w_pa, w_pf, w_pm, w_out, g_post,
           *, tm=512, tn=256):
    T, D = x.shape
    nj = D // tn
    tps = seq // tm
    row = lambda i, j: (i, 0)
    const = lambda i, j: (0, 0)
    colj = lambda i, j: (0, j)
    group_specs = [pl.BlockSpec((1, a.shape[1], tm // a.shape[1], GROUP_WIDTH),
                                lambda i, j: (i // tps, 0, i % tps, 0))
                   for a in (*o_groups, *lse_groups)]
    gate_specs = [pl.BlockSpec((D, tn), lambda i, j, k=k: (0, j + k * nj)) for k in range(3)]
    bias_specs = [pl.BlockSpec((1, tn), lambda i, j, k=k: (0, j + k * nj)) for k in range(3)]
    return pl.pallas_call(
        _merge_body,
        out_shape=jax.ShapeDtypeStruct((T, D), F32),
        grid=(T // tm, nj),
        in_specs=[pl.BlockSpec((tm, D), row), pl.BlockSpec((tm, D), row)]
                 + group_specs
                 + [pl.BlockSpec((tm, FNET_WIDTH), row), pl.BlockSpec((tm, MEM_WIDTH), row)]
                 + gate_specs + bias_specs
                 + [pl.BlockSpec((GROUP_WIDTH, tn), colj), pl.BlockSpec((FNET_WIDTH, tn), colj),
                    pl.BlockSpec((MEM_WIDTH, tn), colj), pl.BlockSpec((tn, D), lambda i, j: (j, 0)),
                    pl.BlockSpec((1, D), const)],
        out_specs=pl.BlockSpec((tm, D), row),
        scratch_shapes=[pltpu.VMEM((tm, GROUP_WIDTH), BF16),
                        pltpu.VMEM((2 * (len(DILATED_GROUPS) - 1) * GROUP_WIDTH // LANES, tm, LANES), F32)],
        compiler_params=_cparams(("parallel", "arbitrary")),
        name="merge",
    )(x, h, *o_groups, *lse_groups, f, m, w_gate, w_gate, w_gate, b_gate, b_gate, b_gate,
      w_pa, w_pf, w_pm, w_out, g_post)


def _layer_params(l, ffn1_norm_pre, ffn1_w_in, ffn1_w_out, ffn1_norm_post, mix_norm_pre, mem_norm, w_in,
                  w_mem_kv, w_gate, b_gate, w_proj_attn, w_proj_fnet, w_proj_mem, w_out, mix_norm_post,
                  ffn2_norm_pre, ffn2_w_in, ffn2_w_out, ffn2_norm_post):
    vec = lambda a: a[l][None, :]
    mat = lambda a: a[l].astype(BF16)
    wi = mat(w_in)
    qkv = lambda g: [wi[:, s * ATTN_WIDTH + g * GROUP_WIDTH: s * ATTN_WIDTH + (g + 1) * GROUP_WIDTH]
                     for s in range(3)]
    w_nat = jnp.concatenate(qkv(0) + [wi[:, 3 * ATTN_WIDTH:]], axis=1)
    w_groups = [jnp.concatenate(qkv(g), axis=1) for g in range(1, len(DILATED_GROUPS))]
    return dict(f1_pre=vec(ffn1_norm_pre), f1_wi=mat(ffn1_w_in), f1_wo=mat(ffn1_w_out),
                f1_post=vec(ffn1_norm_post), mix_pre=vec(mix_norm_pre), mem_norm=vec(mem_norm),
                w_nat=w_nat, w_groups=w_groups, w_mem_kv=mat(w_mem_kv), w_gate=mat(w_gate),
                b_gate=vec(b_gate), w_pa=mat(w_proj_attn), w_pf=mat(w_proj_fnet), w_pm=mat(w_proj_mem),
                w_out=mat(w_out), mix_post=vec(mix_norm_post), f2_pre=vec(ffn2_norm_pre),
                f2_wi=mat(ffn2_w_in), f2_wo=mat(ffn2_w_out), f2_post=vec(ffn2_norm_post))


def _layer(x, mem, bias, wc, ws, p):
    B, S, D = x.shape
    dils = [dil for _, dil in DILATED_GROUPS]
    assert dils[0] == 1
    xt = x.reshape(B * S, D)
    x1, hs = _ffn(xt, S, p["f1_pre"], p["f1_wi"], p["f1_wo"], p["f1_post"], p["mix_pre"], dils)
    h = hs[0].reshape(B * S, D)
    z0 = _in_proj(h, p["w_nat"])
    zs = [z0] + [_in_proj(hd.reshape(B * S, D), w) for hd, w in zip(hs[1:], p["w_groups"])]
    o_groups, lse_groups = [], []
    for g, (dil, z) in enumerate(zip(dils, zs)):
        o, lse = _dilated_group(z.reshape(B, dil, S // dil, z.shape[-1]), bias, g)
        o_groups.append(o)
        lse_groups.append(lse)
    z3 = z0.reshape(B, S, -1)
    f = _fourier(z3, wc, ws)
    m = _memory(z3, mem, p["mem_norm"], p["w_mem_kv"])
    x2 = _merge(x1, S, h, o_groups, lse_groups, f, m, p["w_gate"], p["b_gate"], p["w_pa"], p["w_pf"],
                p["w_pm"], p["w_out"], p["mix_post"])
    x3, _ = _ffn(x2, S, p["f2_pre"], p["f2_wi"], p["f2_wo"], p["f2_post"], p["f2_post"], ())
    return x3.reshape(B, S, D)


def kernel(x_prompt, x_sample, mem_prompt, mem_sample, rel_bias, ffn1_norm_pre, ffn1_w_in, ffn1_w_out, ffn1_norm_post, mix_norm_pre, mem_norm, w_in, w_mem_kv, w_gate, b_gate, w_proj_attn, w_proj_fnet, w_proj_mem, w_out, mix_norm_post, ffn2_norm_pre, ffn2_w_in, ffn2_w_out, ffn2_norm_post):
    depth = w_in.shape[0]
    bias = _bias_tables(rel_bias)
    wc = _dft_table(FNET_GROUP_DIM, False, FNET_GROUP_DIM)
    ws = {s: _dft_table(s, True, 256) for s in {x_prompt.shape[1], x_sample.shape[1]}}
    y_prompt, y_sample = x_prompt, x_sample
    for l in range(depth):
        p = _layer_params(l, ffn1_norm_pre, ffn1_w_in, ffn1_w_out, ffn1_norm_post, mix_norm_pre, mem_norm,
                          w_in, w_mem_kv, w_gate, b_gate, w_proj_attn, w_proj_fnet, w_proj_mem, w_out,
                          mix_norm_post, ffn2_norm_pre, ffn2_w_in, ffn2_w_out, ffn2_norm_post)
        y_prompt = _layer(y_prompt, mem_prompt, bias, wc, ws[y_prompt.shape[1]], p)
        y_sample = _layer(y_sample, mem_sample, bias, wc, ws[y_sample.shape[1]], p)
    return (y_prompt, y_sample)
```

```python
import functools
import math

import jax
import jax.numpy as jnp
from jax import lax
from jax.experimental import pallas as pl
from jax.experimental.pallas import tpu as pltpu

F32 = jnp.float32
BF16 = jnp.bfloat16

RMS_EPS = 1e-6
NEG_INF = -1e30

DILATED_GROUPS = ((128, 1), (512, 4), (2048, 16))
GROUP_HEADS = 4
HEAD_DIM = 128
GROUP_WIDTH = GROUP_HEADS * HEAD_DIM
ATTN_WIDTH = len(DILATED_GROUPS) * GROUP_WIDTH
FNET_GROUPS = 4
FNET_GROUP_DIM = 256
FNET_WIDTH = FNET_GROUPS * FNET_GROUP_DIM
MEM_HEADS = 4
MEM_HEAD_DIM = 256
MEM_WIDTH = MEM_HEADS * MEM_HEAD_DIM
N_BUCKETS = 32
MAX_EXACT = 8
REL_MAX_DIST = 1024

Q_BLK = 128
HALF_WIN = 64
KEY_WIN = 256
COL_BLK = 512
NAT_FNET_COL = 3 * GROUP_WIDTH
NAT_MEM_COL = NAT_FNET_COL + FNET_WIDTH

V7X_VMEM_LIMIT_BYTES = 58 * 1024 * 1024
LANES = 128


def _cparams(semantics, vmem_bytes=V7X_VMEM_LIMIT_BYTES):
    return pltpu.CompilerParams(dimension_semantics=semantics, vmem_limit_bytes=vmem_bytes)


def _rms_f32(x, g):
    return x * lax.rsqrt(jnp.mean(x * x, axis=-1, keepdims=True) + RMS_EPS) * g


ROW_CHUNK = 16


def _for_row_chunks(nrows, fn):
    for k in range(nrows // ROW_CHUNK):
        fn(slice(k * ROW_CHUNK, (k + 1) * ROW_CHUNK))


def _ffn_body(next_dils, x_ref, gpre_ref, wia_ref, wib_ref, wo_ref, gpost_ref, gnext_ref,
              out_ref, *rest):
    hn_refs = rest[:len(next_dils)]
    scratch = rest[len(next_dils):]
    h_scr = scratch[0]
    tm = x_ref.shape[0]
    j = pl.program_id(1)
    nj = pl.num_programs(1)

    def swiglu_out():
        h = h_scr[...]
        a = jnp.dot(h, wia_ref[0], preferred_element_type=F32)
        b = jnp.dot(h, wib_ref[0], preferred_element_type=F32)
        g = (a * jax.nn.sigmoid(a) * b).astype(BF16)
        return jnp.dot(g, wo_ref[...], preferred_element_type=F32)

    @pl.when(j == 0)
    def _():
        gpre = gpre_ref[...]

        def pre(rows):
            h_scr[rows, :] = _rms_f32(x_ref[rows, :], gpre).astype(BF16)

        _for_row_chunks(tm, pre)
        out_ref[...] = swiglu_out()

    @pl.when(j > 0)
    def _():
        out_ref[...] += swiglu_out()

    @pl.when(j == nj - 1)
    def _():
        gpost = 0.5 * gpost_ref[...]
        gnext = gnext_ref[...]

        def post(rows):
            o = x_ref[rows, :] + _rms_f32(out_ref[rows, :], gpost)
            out_ref[rows, :] = o
            if next_dils:
                hn = _rms_f32(o, gnext)
                for hn_ref, dil in zip(hn_refs, next_dils):
                    if dil == 1:
                        hn_ref[0, 0, rows, :] = hn.astype(BF16)
                if len(scratch) > 1:
                    for c in range(scratch[1].shape[0]):
                        scratch[1][c, rows, :] = hn[:, c * LANES:(c + 1) * LANES]

        _for_row_chunks(tm, post)
        prev_scr, prev_dil, level = (scratch[1] if len(scratch) > 1 else None), 1, 2
        for hn_ref, dil in zip(hn_refs, next_dils):
            if dil == 1:
                continue
            ratio, n = dil // prev_dil, tm // dil
            cur_scr = scratch[level] if level < len(scratch) else None
            for rp in range(prev_dil):
                for m in range(ratio):
                    r = rp + prev_dil * m
                    for c in range(prev_scr.shape[0]):
                        piece = prev_scr[c, pl.ds(rp * (tm // prev_dil) + m, n, stride=ratio), :]
                        hn_ref[0, r, :, c * LANES:(c + 1) * LANES] = piece.astype(BF16)
                        if cur_scr is not None:
                            cur_scr[c, r * n:(r + 1) * n, :] = piece
            prev_scr, prev_dil, level = cur_scr, dil, level + 1


def _col_blocks(w, t):
    K, N = w.shape
    return w.reshape(K, N // t, t).transpose(1, 0, 2)


FFN_TF = 512
MERGE_TN = 256


def _ffn(x, seq, g_pre, w_in, w_out, g_post, g_next, next_dils, *, tm=512):
    T, D = x.shape
    F = w_out.shape[0]
    tf = FFN_TF
    nj = F // tf
    assert w_in.shape == (2 * nj, D, tf)
    tps = seq // tm
    row = lambda i, j: (i, 0)
    const = lambda i, j: (0, 0)
    out_shape = [jax.ShapeDtypeStruct((T, D), F32)]
    out_specs = [pl.BlockSpec((tm, D), row)]
    scratch = [pltpu.VMEM((tm, D), BF16)]
    for dil in next_dils:
        out_shape.append(jax.ShapeDtypeStruct((T // seq, dil, seq // dil, D), BF16))
        out_specs.append(pl.BlockSpec((1, dil, tm // dil, D), lambda i, j: (i // tps, 0, i % tps, 0)))
    strided = [dil for dil in next_dils if dil > 1]
    assert all(b % a == 0 for a, b in zip([1] + strided, strided))
    scratch += [pltpu.VMEM((D // LANES, tm, LANES), F32)] * len(strided)
    res = pl.pallas_call(
        functools.partial(_ffn_body, tuple(next_dils)),
        out_shape=out_shape,
        grid=(T // tm, nj),
        in_specs=[
            pl.BlockSpec((tm, D), row),
            pl.BlockSpec((1, D), const),
            pl.BlockSpec((1, D, tf), lambda i, j: (j, 0, 0)),
            pl.BlockSpec((1, D, tf), lambda i, j: (j + nj, 0, 0)),
            pl.BlockSpec((tf, D), lambda i, j: (j, 0)),
            pl.BlockSpec((1, D), const),
            pl.BlockSpec((1, D), const),
        ],
        out_specs=out_specs,
        scratch_shapes=scratch,
        compiler_params=_cparams(("parallel", "arbitrary")),
        name="ffn",
    )(x, g_pre, w_in, w_in, w_out, g_post, g_next)
    return res[0], res[1:]


def _matmul_body(h_ref, w_ref, o_ref):
    o_ref[...] = jnp.dot(h_ref[...], w_ref[...], preferred_element_type=F32).astype(o_ref.dtype)


def _in_proj(h, w, *, tm=1024, tn_max=1792):
    T, D = h.shape
    N = w.shape[1]
    tn = min(N, tn_max)
    return pl.pallas_call(
        _matmul_body,
        out_shape=jax.ShapeDtypeStruct((T, N), BF16),
        grid=(N // tn, T // tm),
        in_specs=[pl.BlockSpec((tm, D), lambda n, i: (i, 0)),
                  pl.BlockSpec((D, tn), lambda n, i: (0, n))],
        out_specs=pl.BlockSpec((tm, tn), lambda n, i: (i, n)),
        compiler_params=_cparams(("parallel", "parallel")),
        name="in_proj",
    )(h, w)


def _t5_bucket(rel):
    half = N_BUCKETS // 2
    n = jnp.abs(rel)
    nf = jnp.maximum(n, 1).astype(F32)
    large = MAX_EXACT + (jnp.log(nf / MAX_EXACT) / math.log(REL_MAX_DIST / MAX_EXACT)
                         * (half - MAX_EXACT)).astype(jnp.int32)
    large = jnp.minimum(large, half - 1)
    return jnp.where(rel > 0, half, 0) + jnp.where(n < MAX_EXACT, n, large)


WIN_OFFSETS = (-HALF_WIN, 0, Q_BLK - KEY_WIN)


def _bias_body(tbl_ref, out_ref):
    g = pl.program_id(0)
    w = pl.program_id(1)
    dil = jnp.where(g == 0, DILATED_GROUPS[0][1],
                    jnp.where(g == 1, DILATED_GROUPS[1][1], DILATED_GROUPS[2][1]))
    off = jnp.where(w == 0, WIN_OFFSETS[0], jnp.where(w == 1, WIN_OFFSETS[1], WIN_OFFSETS[2]))
    s = lax.broadcasted_iota(jnp.int32, (Q_BLK, KEY_WIN), 0)
    t = lax.broadcasted_iota(jnp.int32, (Q_BLK, KEY_WIN), 1)
    delta = t - s + off
    bucket = _t5_bucket(delta * dil)
    valid = jnp.abs(delta) <= HALF_WIN
    for h in range(GROUP_HEADS):
        col = g * GROUP_HEADS + h

        def pick(i, acc):
            return jnp.where(bucket == i, tbl_ref[i, col], acc)

        bias = lax.fori_loop(0, N_BUCKETS, pick, jnp.zeros((Q_BLK, KEY_WIN), F32))
        out_ref[0, 0, h] = jnp.where(valid, bias, NEG_INF)


def _bias_tables(rel_bias):
    ng = len(DILATED_GROUPS)
    return pl.pallas_call(
        _bias_body,
        out_shape=jax.ShapeDtypeStruct((ng, len(WIN_OFFSETS), GROUP_HEADS, Q_BLK, KEY_WIN), F32),
        grid=(ng, len(WIN_OFFSETS)),
        in_specs=[pl.BlockSpec(memory_space=pltpu.SMEM)],
        out_specs=pl.BlockSpec((1, 1, GROUP_HEADS, Q_BLK, KEY_WIN), lambda g, w: (g, w, 0, 0, 0)),
        compiler_params=_cparams(("arbitrary", "arbitrary")),
        name="rel_bias",
    )(rel_bias)


def _attn_body(L, W, TL, q_ref, k_ref, v_ref, bias_ref, o_ref, lse_ref):
    scale = HEAD_DIM ** -0.5
    nblk = L // Q_BLK
    per_res = TL // Q_BLK
    n0 = pl.program_id(2) * per_res
    ones = jnp.ones((W, HEAD_DIM), BF16)

    def block(i, carry):
        rb = i // per_res
        n = n0 + i % per_res
        qs = pl.multiple_of((i % per_res) * Q_BLK, Q_BLK)
        ks = pl.multiple_of(jnp.clip(n * Q_BLK - HALF_WIN, 0, L - W), HALF_WIN)
        kind = jnp.where(n == 0, 1, jnp.where(n == nblk - 1, 2, 0))
        for h in range(GROUP_HEADS):
            cols = slice(h * HEAD_DIM, (h + 1) * HEAD_DIM)
            qh = q_ref[0, rb, pl.ds(qs, Q_BLK), cols]
            kh = k_ref[0, rb, pl.ds(ks, W), cols]
            vh = v_ref[0, rb, pl.ds(ks, W), cols]
            s = lax.dot_general(qh, kh, (((1,), (1,)), ((), ())), preferred_element_type=F32)
            s = s * scale + bias_ref[0, kind, h, :, :W]
            m = jnp.max(s, axis=-1, keepdims=True)
            p = jnp.exp(s - m).astype(BF16)
            od = jnp.dot(p, jnp.concatenate([vh, ones], axis=1), preferred_element_type=F32)
            den = od[:, HEAD_DIM:]
            o_ref[0, rb, pl.ds(qs, Q_BLK), cols] = od[:, :HEAD_DIM] / den
            lse_ref[0, rb, pl.ds(qs, Q_BLK), cols] = m + jnp.log(den)
        return carry

    lax.fori_loop(0, q_ref.shape[1] * per_res, block, 0, unroll=4)


ATTN_ROWS_PER_STEP = 1024


def _dilated_group(zg, bias, g):
    B, dil, L, _ = zg.shape
    W = min(KEY_WIN, L)
    TL = min(ATTN_ROWS_PER_STEP, L)
    RB = min(dil, ATTN_ROWS_PER_STEP // TL)
    kv_blk = (1, RB, L, GROUP_WIDTH)
    q_blk = (1, RB, TL, GROUP_WIDTH)
    out_sds = jax.ShapeDtypeStruct((B, dil, L, GROUP_WIDTH), F32)
    return pl.pallas_call(
        functools.partial(_attn_body, L, W, TL),
        out_shape=[out_sds, out_sds],
        grid=(B, dil // RB, L // TL),
        in_specs=[
            pl.BlockSpec(q_blk, lambda b, r, c: (b, r, c, 0)),
            pl.BlockSpec(kv_blk, lambda b, r, c: (b, r, 0, 1)),
            pl.BlockSpec(kv_blk, lambda b, r, c: (b, r, 0, 2)),
            pl.BlockSpec((1, len(WIN_OFFSETS), GROUP_HEADS, Q_BLK, W), lambda b, r, c: (g, 0, 0, 0, 0)),
        ],
        out_specs=[pl.BlockSpec(q_blk, lambda b, r, c: (b, r, c, 0)),
                   pl.BlockSpec(q_blk, lambda b, r, c: (b, r, c, 0))],
        compiler_params=_cparams(("parallel", "parallel", "arbitrary")),
        name=f"dilated_attn_g{g}",
    )(zg, zg, zg, bias)


def _dft_angle(r, c, n):
    return ((r * c) & (n - 1)).astype(F32) * (2.0 * math.pi / n)


def _dft_base_body(n, cos_ref, sin_ref):
    rows = cos_ref.shape[0]
    ang = _dft_angle(lax.broadcasted_iota(jnp.int32, (rows, n), 0),
                     lax.broadcasted_iota(jnp.int32, (rows, n), 1), n)
    cos_ref[...] = jnp.cos(ang)
    sin_ref[...] = jnp.sin(ang)


def _dft_table_body(n, neg_sin, cb_ref, sb_ref, o_ref):
    rows = o_ref.shape[0]
    ang0 = _dft_angle(pl.program_id(0) * rows, lax.broadcasted_iota(jnp.int32, (1, n), 1), n)
    c0, s0 = jnp.cos(ang0), jnp.sin(ang0)
    cb, sb = cb_ref[...], sb_ref[...]
    sn = s0 * cb + c0 * sb
    o_ref[:, :n] = (c0 * cb - s0 * sb).astype(o_ref.dtype)
    o_ref[:, n:] = (-sn if neg_sin else sn).astype(o_ref.dtype)


def _dft_table(n, neg_sin, rows):
    assert n & (n - 1) == 0
    base_sds = jax.ShapeDtypeStruct((rows, n), F32)
    cb, sb = pl.pallas_call(
        functools.partial(_dft_base_body, n),
        out_shape=[base_sds, base_sds],
        compiler_params=_cparams(()),
        name=f"dft_base_{n}",
    )()
    return pl.pallas_call(
        functools.partial(_dft_table_body, n, neg_sin),
        out_shape=jax.ShapeDtypeStruct((n, 2 * n), BF16),
        grid=(n // rows,),
        in_specs=[pl.BlockSpec((rows, n), lambda i: (0, 0))] * 2,
        out_specs=pl.BlockSpec((rows, 2 * n), lambda i: (i, 0)),
        compiler_params=_cparams(("parallel",)),
        name=f"dft_table_{n}",
    )(cb, sb)


def _fourier_body(S, rows1, ua_ref, ub_ref, wc_ref, ws_ref, o_ref, y_scr):
    C = FNET_GROUP_DIM

    @pl.when(pl.program_id(1) == 0)
    def _():
        for u_ref, base in ((ua_ref, 0), (ub_ref, COL_BLK)):
            for k in range(COL_BLK // C):
                for r0 in range(0, S, rows1):
                    y = jnp.dot(u_ref[0, r0:r0 + rows1, k * C:(k + 1) * C], wc_ref[...],
                                preferred_element_type=F32)
                    c0 = base + k * C
                    y_scr[r0:r0 + rows1, c0:c0 + C] = y[:, :C].astype(BF16)
                    y_scr[S + r0:S + r0 + rows1, c0:c0 + C] = y[:, C:].astype(BF16)

    scale = 1.0 / math.sqrt(S * C)
    f = jnp.dot(ws_ref[...], y_scr[...], preferred_element_type=F32) * scale
    o_ref[0] = f.astype(o_ref.dtype)


def _fourier(z3, wc, ws, *, tr=256, rows1=512):
    B, S, C = z3.shape
    cb = NAT_FNET_COL // COL_BLK
    return pl.pallas_call(
        functools.partial(_fourier_body, S, rows1),
        out_shape=jax.ShapeDtypeStruct((B, S, FNET_WIDTH), BF16),
        grid=(B, S // tr),
        in_specs=[
            pl.BlockSpec((1, S, COL_BLK), lambda b, i: (b, 0, cb)),
            pl.BlockSpec((1, S, COL_BLK), lambda b, i: (b, 0, cb + 1)),
            pl.BlockSpec((FNET_GROUP_DIM, 2 * FNET_GROUP_DIM), lambda b, i: (0, 0)),
            pl.BlockSpec((tr, 2 * S), lambda b, i: (i, 0)),
        ],
        out_specs=pl.BlockSpec((1, tr, FNET_WIDTH), lambda b, i: (b, i, 0)),
        scratch_shapes=[pltpu.VMEM((2 * S, FNET_WIDTH), BF16)],
        compiler_params=_cparams(("parallel", "arbitrary")),
        name="fourier",
    )(z3, z3, wc, ws).reshape(B * S, FNET_WIDTH)


def _memory_body(qa_ref, qb_ref, mem_ref, gm_ref, wkv_ref, o_ref, k_scr, v_scr):
    E = MEM_HEAD_DIM

    @pl.when(pl.program_id(1) == 0)
    def _():
        hm = _rms_f32(mem_ref[0], gm_ref[...]).astype(BF16)
        kv = jnp.dot(hm, wkv_ref[...], preferred_element_type=F32)
        k_scr[...] = (kv[:, :MEM_WIDTH] * (E ** -0.5)).astype(BF16)
        v_scr[...] = kv[:, MEM_WIDTH:].astype(BF16)

    for q_ref, base in ((qa_ref, 0), (qb_ref, COL_BLK)):
        for k in range(COL_BLK // E):
            c0 = base + k * E
            s = lax.dot_general(q_ref[0, :, k * E:(k + 1) * E], k_scr[:, c0:c0 + E],
                                (((1,), (1,)), ((), ())), preferred_element_type=F32)
            m = jnp.max(s, axis=-1, keepdims=True)
            p = jnp.exp(s - m)
            den = jnp.sum(p, axis=-1, keepdims=True)
            o = jnp.dot(p.astype(BF16), v_scr[:, c0:c0 + E], preferred_element_type=F32) / den
            o_ref[0, :, c0:c0 + E] = o.astype(o_ref.dtype)


def _memory(z3, mem, g_mem, w_kv, *, tq=1024):
    B, S, C = z3.shape
    M, D = mem.shape[1:]
    cb = NAT_MEM_COL // COL_BLK
    assert MEM_HEAD_DIM ** -0.5 == 2.0 ** round(math.log2(MEM_HEAD_DIM ** -0.5))
    return pl.pallas_call(
        _memory_body,
        out_shape=jax.ShapeDtypeStruct((B, S, MEM_WIDTH), BF16),
        grid=(B, S // tq),
        in_specs=[
            pl.BlockSpec((1, tq, COL_BLK), lambda b, i: (b, i, cb)),
            pl.BlockSpec((1, tq, COL_BLK), lambda b, i: (b, i, cb + 1)),
            pl.BlockSpec((1, M, D), lambda b, i: (b, 0, 0)),
            pl.BlockSpec((1, D), lambda b, i: (0, 0)),
            pl.BlockSpec((D, 2 * MEM_WIDTH), lambda b, i: (0, 0)),
        ],
        out_specs=pl.BlockSpec((1, tq, MEM_WIDTH), lambda b, i: (b, i, 0)),
        scratch_shapes=[pltpu.VMEM((M, MEM_WIDTH), BF16), pltpu.VMEM((M, MEM_WIDTH), BF16)],
        compiler_params=_cparams(("parallel", "arbitrary")),
        name="memory_attn",
    )(z3, z3, mem, g_mem, w_kv).reshape(B * S, MEM_WIDTH)


def _merge_body(x_ref, h_ref, o0_ref, o1_ref, o2_ref, l0_ref, l1_ref, l2_ref, f_ref, m_ref,
                wg0_ref, wg1_ref, wg2_ref, bg0_ref, bg1_ref, bg2_ref, wpa_ref, wpf_ref, wpm_ref,
                wo_ref, gpost_ref, out_ref, a_scr, nat_scr):
    j = pl.program_id(1)
    nj = pl.num_programs(1)
    tm = x_ref.shape[0]

    nc = GROUP_WIDTH // LANES

    def to_natural(ref, slot):
        dil = ref.shape[1]
        for r in range(dil):
            for c in range(nc):
                nat_scr[slot * nc + c, pl.ds(r, tm // dil, stride=dil), :] = (
                    ref[0, r, :, c * LANES:(c + 1) * LANES])

    def natural_rows(ref, slot, rows):
        if ref.shape[1] == 1:
            return ref[0, 0, rows, :]
        return jnp.concatenate([nat_scr[slot * nc + c, rows, :] for c in range(nc)], axis=1)

    def merged_out():
        h = h_ref[...]

        def gate(w_ref, b_ref):
            return jax.nn.sigmoid(jnp.dot(h, w_ref[0], preferred_element_type=F32) + b_ref[...])

        mg = gate(wg0_ref, bg0_ref) * jnp.dot(a_scr[...], wpa_ref[0], preferred_element_type=F32)
        mg += gate(wg1_ref, bg1_ref) * jnp.dot(f_ref[...], wpf_ref[0], preferred_element_type=F32)
        mg += gate(wg2_ref, bg2_ref) * jnp.dot(m_ref[...], wpm_ref[0], preferred_element_type=F32)
        return jnp.dot(mg.astype(BF16), wo_ref[...], preferred_element_type=F32)

    @pl.when(j == 0)
    def _():
        slots = {}
        for ref in (l1_ref, l2_ref, o1_ref, o2_ref):
            slots[id(ref)] = len(slots)
            to_natural(ref, slots[id(ref)])

        def combine(rows):
            l0 = natural_rows(l0_ref, None, rows)
            l1 = natural_rows(l1_ref, slots[id(l1_ref)], rows)
            l2 = natural_rows(l2_ref, slots[id(l2_ref)], rows)
            mx = jnp.maximum(jnp.maximum(l0, l1), l2)
            e0, e1, e2 = jnp.exp(l0 - mx), jnp.exp(l1 - mx), jnp.exp(l2 - mx)
            num = (e0 * natural_rows(o0_ref, None, rows)
                   + e1 * natural_rows(o1_ref, slots[id(o1_ref)], rows)
                   + e2 * natural_rows(o2_ref, slots[id(o2_ref)], rows))
            a_scr[rows, :] = (num / (e0 + e1 + e2)).astype(BF16)

        _for_row_chunks(tm, combine)
        out_ref[...] = merged_out()

    @pl.when(j > 0)
    def _():
        out_ref[...] += merged_out()

    @pl.when(j == nj - 1)
    def _():
        gpost = gpost_ref[...]

        def post(rows):
            out_ref[rows, :] = x_ref[rows, :] + _rms_f32(out_ref[rows, :], gpost)

        _for_row_chunks(tm, post)


def _merge(x, seq, h, o_groups, lse_groups, f, m, w_gate, b_gate, w_pa, w_pf, w_pm, w_out, g_post,
           *, tm=512):
    T, D = x.shape
    tn = MERGE_TN
    nj = D // tn
    assert w_gate.shape == (3 * nj, D, tn)
    tps = seq // tm
    row = lambda i, j: (i, 0)
    const = lambda i, j: (0, 0)
    colj = lambda i, j: (j, 0, 0)
    group_specs = [pl.BlockSpec((1, a.shape[1], tm // a.shape[1], GROUP_WIDTH),
                                lambda i, j: (i // tps, 0, i % tps, 0))
                   for a in (*o_groups, *lse_groups)]
    gate_specs = [pl.BlockSpec((1, D, tn), lambda i, j, k=k: (j + k * nj, 0, 0)) for k in range(3)]
    bias_specs = [pl.BlockSpec((1, tn), lambda i, j, k=k: (0, j + k * nj)) for k in range(3)]
    return pl.pallas_call(
        _merge_body,
        out_shape=jax.ShapeDtypeStruct((T, D), F32),
        grid=(T // tm, nj),
        in_specs=[pl.BlockSpec((tm, D), row), pl.BlockSpec((tm, D), row)]
                 + group_specs
                 + [pl.BlockSpec((tm, FNET_WIDTH), row), pl.BlockSpec((tm, MEM_WIDTH), row)]
                 + gate_specs + bias_specs
                 + [pl.BlockSpec((1, GROUP_WIDTH, tn), colj), pl.BlockSpec((1, FNET_WIDTH, tn), colj),
                    pl.BlockSpec((1, MEM_WIDTH, tn), colj), pl.BlockSpec((tn, D), lambda i, j: (j, 0)),
                    pl.BlockSpec((1, D), const)],
        out_specs=pl.BlockSpec((tm, D), row),
        scratch_shapes=[pltpu.VMEM((tm, GROUP_WIDTH), BF16),
                        pltpu.VMEM((2 * (len(DILATED_GROUPS) - 1) * GROUP_WIDTH // LANES, tm, LANES), F32)],
        compiler_params=_cparams(("parallel", "arbitrary")),
        name="merge",
    )(x, h, *o_groups, *lse_groups, f, m, w_gate, w_gate, w_gate, b_gate, b_gate, b_gate,
      w_pa, w_pf, w_pm, w_out, g_post)


def _layer_params(l, ffn1_norm_pre, ffn1_w_in, ffn1_w_out, ffn1_norm_post, mix_norm_pre, mem_norm, w_in,
                  w_mem_kv, w_gate, b_gate, w_proj_attn, w_proj_fnet, w_proj_mem, w_out, mix_norm_post,
                  ffn2_norm_pre, ffn2_w_in, ffn2_w_out, ffn2_norm_post):
    vec = lambda a: a[l][None, :]
    mat = lambda a: a[l].astype(BF16)
    wi = mat(w_in)
    qkv = lambda g: [wi[:, s * ATTN_WIDTH + g * GROUP_WIDTH: s * ATTN_WIDTH + (g + 1) * GROUP_WIDTH]
                     for s in range(3)]
    w_nat = jnp.concatenate(qkv(0) + [wi[:, 3 * ATTN_WIDTH:]], axis=1)
    w_groups = [jnp.concatenate(qkv(g), axis=1) for g in range(1, len(DILATED_GROUPS))]
    ffn_blocks = lambda a: _col_blocks(mat(a), FFN_TF)
    merge_blocks = lambda a: _col_blocks(mat(a), MERGE_TN)
    return dict(f1_pre=vec(ffn1_norm_pre), f1_wi=ffn_blocks(ffn1_w_in), f1_wo=mat(ffn1_w_out),
                f1_post=vec(ffn1_norm_post), mix_pre=vec(mix_norm_pre), mem_norm=vec(mem_norm),
                w_nat=w_nat, w_groups=w_groups, w_mem_kv=mat(w_mem_kv), w_gate=merge_blocks(w_gate),
                b_gate=vec(b_gate), w_pa=merge_blocks(w_proj_attn), w_pf=merge_blocks(w_proj_fnet),
                w_pm=merge_blocks(w_proj_mem), w_out=mat(w_out), mix_post=vec(mix_norm_post),
                f2_pre=vec(ffn2_norm_pre), f2_wi=ffn_blocks(ffn2_w_in), f2_wo=mat(ffn2_w_out),
                f2_post=vec(ffn2_norm_post))


def _layer(x, mem, bias, wc, ws, p):
    B, S, D = x.shape
    dils = [dil for _, dil in DILATED_GROUPS]
    assert dils[0] == 1
    xt = x.reshape(B * S, D)
    x1, hs = _ffn(xt, S, p["f1_pre"], p["f1_wi"], p["f1_wo"], p["f1_post"], p["mix_pre"], dils)
    h = hs[0].reshape(B * S, D)
    z0 = _in_proj(h, p["w_nat"])
    zs = [z0] + [_in_proj(hd.reshape(B * S, D), w) for hd, w in zip(hs[1:], p["w_groups"])]
    o_groups, lse_groups = [], []
    for g, (dil, z) in enumerate(zip(dils, zs)):
        o, lse = _dilated_group(z.reshape(B, dil, S // dil, z.shape[-1]), bias, g)
        o_groups.append(o)
        lse_groups.append(lse)
    z3 = z0.reshape(B, S, -1)
    f = _fourier(z3, wc, ws)
    m = _memory(z3, mem, p["mem_norm"], p["w_mem_kv"])
    x2 = _merge(x1, S, h, o_groups, lse_groups, f, m, p["w_gate"], p["b_gate"], p["w_pa"], p["w_pf"],
                p["w_pm"], p["w_out"], p["mix_post"])
    x3, _ = _ffn(x2, S, p["f2_pre"], p["f2_wi"], p["f2_wo"], p["f2_post"], p["f2_post"], ())
    return x3.reshape(B, S, D)


def kernel(x_prompt, x_sample, mem_prompt, mem_sample, rel_bias, ffn1_norm_pre, ffn1_w_in, ffn1_w_out, ffn1_norm_post, mix_norm_pre, mem_norm, w_in, w_mem_kv, w_gate, b_gate, w_proj_attn, w_proj_fnet, w_proj_mem, w_out, mix_norm_post, ffn2_norm_pre, ffn2_w_in, ffn2_w_out, ffn2_norm_post):
    depth = w_in.shape[0]
    bias = _bias_tables(rel_bias)
    wc = _dft_table(FNET_GROUP_DIM, False, FNET_GROUP_DIM)
    ws = {s: _dft_table(s, True, 256) for s in {x_prompt.shape[1], x_sample.shape[1]}}
    y_prompt, y_sample = x_prompt, x_sample
    for l in range(depth):
        p = _layer_params(l, ffn1_norm_pre, ffn1_w_in, ffn1_w_out, ffn1_norm_post, mix_norm_pre, mem_norm,
                          w_in, w_mem_kv, w_gate, b_gate, w_proj_attn, w_proj_fnet, w_proj_mem, w_out,
                          mix_norm_post, ffn2_norm_pre, ffn2_w_in, ffn2_w_out, ffn2_norm_post)
        y_prompt = _layer(y_prompt, mem_prompt, bias, wc, ws[y_prompt.shape[1]], p)
        y_sample = _layer(y_sample, mem_sample, bias, wc, ws[y_sample.shape[1]], p)
    return (y_prompt, y_sample)
```

```python
import functools
import math

import jax
import jax.numpy as jnp
from jax import lax
from jax.experimental import pallas as pl
from jax.experimental.pallas import tpu as pltpu

F32 = jnp.float32
BF16 = jnp.bfloat16

RMS_EPS = 1e-6
NEG_INF = -1e30

DILATED_GROUPS = ((128, 1), (512, 4), (2048, 16))
GROUP_HEADS = 4
HEAD_DIM = 128
GROUP_WIDTH = GROUP_HEADS * HEAD_DIM
ATTN_WIDTH = len(DILATED_GROUPS) * GROUP_WIDTH
FNET_GROUPS = 4
FNET_GROUP_DIM = 256
FNET_WIDTH = FNET_GROUPS * FNET_GROUP_DIM
MEM_HEADS = 4
MEM_HEAD_DIM = 256
MEM_WIDTH = MEM_HEADS * MEM_HEAD_DIM
N_BUCKETS = 32
MAX_EXACT = 8
REL_MAX_DIST = 1024

Q_BLK = 128
HALF_WIN = 64
KEY_WIN = 256
COL_BLK = 512
NAT_FNET_COL = 3 * GROUP_WIDTH
NAT_MEM_COL = NAT_FNET_COL + FNET_WIDTH

V7X_VMEM_LIMIT_BYTES = 58 * 1024 * 1024
LANES = 128


def _cparams(semantics, vmem_bytes=V7X_VMEM_LIMIT_BYTES):
    return pltpu.CompilerParams(dimension_semantics=semantics, vmem_limit_bytes=vmem_bytes)


def _rms_f32(x, g):
    return x * lax.rsqrt(jnp.mean(x * x, axis=-1, keepdims=True) + RMS_EPS) * g


ROW_CHUNK = 16


def _for_row_chunks(nrows, fn):
    for k in range(nrows // ROW_CHUNK):
        fn(slice(k * ROW_CHUNK, (k + 1) * ROW_CHUNK))


def _ffn_body(next_dils, x_ref, gpre_ref, wia_ref, wib_ref, wo_ref, gpost_ref, gnext_ref,
              out_ref, *rest):
    hn_refs = rest[:len(next_dils)]
    scratch = rest[len(next_dils):]
    h_scr = scratch[0]
    tm = x_ref.shape[0]
    j = pl.program_id(1)
    nj = pl.num_programs(1)

    def swiglu_out():
        h = h_scr[...]
        a = jnp.dot(h, wia_ref[...], preferred_element_type=F32)
        b = jnp.dot(h, wib_ref[...], preferred_element_type=F32)
        g = (a * jax.nn.sigmoid(a) * b).astype(BF16)
        return jnp.dot(g, wo_ref[...], preferred_element_type=F32)

    @pl.when(j == 0)
    def _():
        gpre = gpre_ref[...]

        def pre(rows):
            h_scr[rows, :] = _rms_f32(x_ref[rows, :], gpre).astype(BF16)

        _for_row_chunks(tm, pre)
        out_ref[...] = swiglu_out()

    @pl.when(j > 0)
    def _():
        out_ref[...] += swiglu_out()

    @pl.when(j == nj - 1)
    def _():
        gpost = 0.5 * gpost_ref[...]
        gnext = gnext_ref[...]

        def post(rows):
            o = x_ref[rows, :] + _rms_f32(out_ref[rows, :], gpost)
            out_ref[rows, :] = o
            if next_dils:
                hn = _rms_f32(o, gnext)
                for hn_ref, dil in zip(hn_refs, next_dils):
                    if dil == 1:
                        hn_ref[0, 0, rows, :] = hn.astype(BF16)
                if len(scratch) > 1:
                    for c in range(scratch[1].shape[0]):
                        scratch[1][c, rows, :] = hn[:, c * LANES:(c + 1) * LANES]

        _for_row_chunks(tm, post)
        prev_scr, prev_dil, level = (scratch[1] if len(scratch) > 1 else None), 1, 2
        for hn_ref, dil in zip(hn_refs, next_dils):
            if dil == 1:
                continue
            ratio, n = dil // prev_dil, tm // dil
            cur_scr = scratch[level] if level < len(scratch) else None
            for rp in range(prev_dil):
                for m in range(ratio):
                    r = rp + prev_dil * m
                    for c in range(prev_scr.shape[0]):
                        piece = prev_scr[c, pl.ds(rp * (tm // prev_dil) + m, n, stride=ratio), :]
                        hn_ref[0, r, :, c * LANES:(c + 1) * LANES] = piece.astype(BF16)
                        if cur_scr is not None:
                            cur_scr[c, r * n:(r + 1) * n, :] = piece
            prev_scr, prev_dil, level = cur_scr, dil, level + 1


def _ffn(x, seq, g_pre, w_in, w_out, g_post, g_next, next_dils, *, tm=512, tf=512):
    T, D = x.shape
    F = w_out.shape[0]
    nj = F // tf
    tps = seq // tm
    row = lambda i, j: (i, 0)
    const = lambda i, j: (0, 0)
    out_shape = [jax.ShapeDtypeStruct((T, D), F32)]
    out_specs = [pl.BlockSpec((tm, D), row)]
    scratch = [pltpu.VMEM((tm, D), BF16)]
    for dil in next_dils:
        out_shape.append(jax.ShapeDtypeStruct((T // seq, dil, seq // dil, D), BF16))
        out_specs.append(pl.BlockSpec((1, dil, tm // dil, D), lambda i, j: (i // tps, 0, i % tps, 0)))
    strided = [dil for dil in next_dils if dil > 1]
    assert all(b % a == 0 for a, b in zip([1] + strided, strided))
    scratch += [pltpu.VMEM((D // LANES, tm, LANES), F32)] * len(strided)
    res = pl.pallas_call(
        functools.partial(_ffn_body, tuple(next_dils)),
        out_shape=out_shape,
        grid=(T // tm, nj),
        in_specs=[
            pl.BlockSpec((tm, D), row),
            pl.BlockSpec((1, D), const),
            pl.BlockSpec((D, tf), lambda i, j: (0, j)),
            pl.BlockSpec((D, tf), lambda i, j: (0, j + nj)),
            pl.BlockSpec((tf, D), lambda i, j: (j, 0)),
            pl.BlockSpec((1, D), const),
            pl.BlockSpec((1, D), const),
        ],
        out_specs=out_specs,
        scratch_shapes=scratch,
        compiler_params=_cparams(("parallel", "arbitrary")),
        name="ffn",
    )(x, g_pre, w_in, w_in, w_out, g_post, g_next)
    return res[0], res[1:]


def _matmul_body(h_ref, w_ref, o_ref):
    o_ref[...] = jnp.dot(h_ref[...], w_ref[...], preferred_element_type=F32).astype(o_ref.dtype)


def _in_proj(h, w, *, tm=1024, tn_max=1792):
    T, D = h.shape
    N = w.shape[1]
    tn = min(N, tn_max)
    return pl.pallas_call(
        _matmul_body,
        out_shape=jax.ShapeDtypeStruct((T, N), BF16),
        grid=(N // tn, T // tm),
        in_specs=[pl.BlockSpec((tm, D), lambda n, i: (i, 0)),
                  pl.BlockSpec((D, tn), lambda n, i: (0, n))],
        out_specs=pl.BlockSpec((tm, tn), lambda n, i: (i, n)),
        compiler_params=_cparams(("parallel", "parallel")),
        name="in_proj",
    )(h, w)


def _t5_bucket(rel):
    half = N_BUCKETS // 2
    n = jnp.abs(rel)
    nf = jnp.maximum(n, 1).astype(F32)
    large = MAX_EXACT + (jnp.log(nf / MAX_EXACT) / math.log(REL_MAX_DIST / MAX_EXACT)
                         * (half - MAX_EXACT)).astype(jnp.int32)
    large = jnp.minimum(large, half - 1)
    return jnp.where(rel > 0, half, 0) + jnp.where(n < MAX_EXACT, n, large)


WIN_OFFSETS = (-HALF_WIN, 0, Q_BLK - KEY_WIN)


def _bias_body(tbl_ref, out_ref):
    g = pl.program_id(0)
    w = pl.program_id(1)
    dil = jnp.where(g == 0, DILATED_GROUPS[0][1],
                    jnp.where(g == 1, DILATED_GROUPS[1][1], DILATED_GROUPS[2][1]))
    off = jnp.where(w == 0, WIN_OFFSETS[0], jnp.where(w == 1, WIN_OFFSETS[1], WIN_OFFSETS[2]))
    s = lax.broadcasted_iota(jnp.int32, (Q_BLK, KEY_WIN), 0)
    t = lax.broadcasted_iota(jnp.int32, (Q_BLK, KEY_WIN), 1)
    delta = t - s + off
    bucket = _t5_bucket(delta * dil)
    valid = jnp.abs(delta) <= HALF_WIN
    for h in range(GROUP_HEADS):
        col = g * GROUP_HEADS + h

        def pick(i, acc):
            return jnp.where(bucket == i, tbl_ref[i, col], acc)

        bias = lax.fori_loop(0, N_BUCKETS, pick, jnp.zeros((Q_BLK, KEY_WIN), F32))
        out_ref[0, 0, h] = jnp.where(valid, bias, NEG_INF)


def _bias_tables(rel_bias):
    ng = len(DILATED_GROUPS)
    return pl.pallas_call(
        _bias_body,
        out_shape=jax.ShapeDtypeStruct((ng, len(WIN_OFFSETS), GROUP_HEADS, Q_BLK, KEY_WIN), F32),
        grid=(ng, len(WIN_OFFSETS)),
        in_specs=[pl.BlockSpec(memory_space=pltpu.SMEM)],
        out_specs=pl.BlockSpec((1, 1, GROUP_HEADS, Q_BLK, KEY_WIN), lambda g, w: (g, w, 0, 0, 0)),
        compiler_params=_cparams(("arbitrary", "arbitrary")),
        name="rel_bias",
    )(rel_bias)


def _attn_body(L, W, TL, q_ref, k_ref, v_ref, bias_ref, o_ref, lse_ref):
    scale = HEAD_DIM ** -0.5
    nblk = L // Q_BLK
    per_res = TL // Q_BLK
    n0 = pl.program_id(2) * per_res
    ones = jnp.ones((W, HEAD_DIM), BF16)

    def block(i, carry):
        rb = i // per_res
        n = n0 + i % per_res
        qs = pl.multiple_of((i % per_res) * Q_BLK, Q_BLK)
        ks = pl.multiple_of(jnp.clip(n * Q_BLK - HALF_WIN, 0, L - W), HALF_WIN)
        kind = jnp.where(n == 0, 1, jnp.where(n == nblk - 1, 2, 0))
        for h in range(GROUP_HEADS):
            cols = slice(h * HEAD_DIM, (h + 1) * HEAD_DIM)
            qh = q_ref[0, rb, pl.ds(qs, Q_BLK), cols]
            kh = k_ref[0, rb, pl.ds(ks, W), cols]
            vh = v_ref[0, rb, pl.ds(ks, W), cols]
            s = lax.dot_general(qh, kh, (((1,), (1,)), ((), ())), preferred_element_type=F32)
            s = s * scale + bias_ref[0, kind, h, :, :W]
            m = jnp.max(s, axis=-1, keepdims=True)
            p = jnp.exp(s - m).astype(BF16)
            od = jnp.dot(p, jnp.concatenate([vh, ones], axis=1), preferred_element_type=F32)
            den = od[:, HEAD_DIM:]
            o_ref[0, rb, pl.ds(qs, Q_BLK), cols] = od[:, :HEAD_DIM] / den
            lse_ref[0, rb, pl.ds(qs, Q_BLK), cols] = m + jnp.log(den)
        return carry

    lax.fori_loop(0, q_ref.shape[1] * per_res, block, 0, unroll=4)


ATTN_ROWS_PER_STEP = 1024


def _dilated_group(zg, bias, g):
    B, dil, L, _ = zg.shape
    W = min(KEY_WIN, L)
    TL = min(ATTN_ROWS_PER_STEP, L)
    RB = min(dil, ATTN_ROWS_PER_STEP // TL)
    kv_blk = (1, RB, L, GROUP_WIDTH)
    q_blk = (1, RB, TL, GROUP_WIDTH)
    out_sds = jax.ShapeDtypeStruct((B, dil, L, GROUP_WIDTH), F32)
    return pl.pallas_call(
        functools.partial(_attn_body, L, W, TL),
        out_shape=[out_sds, out_sds],
        grid=(B, dil // RB, L // TL),
        in_specs=[
            pl.BlockSpec(q_blk, lambda b, r, c: (b, r, c, 0)),
            pl.BlockSpec(kv_blk, lambda b, r, c: (b, r, 0, 1)),
            pl.BlockSpec(kv_blk, lambda b, r, c: (b, r, 0, 2)),
            pl.BlockSpec((1, len(WIN_OFFSETS), GROUP_HEADS, Q_BLK, W), lambda b, r, c: (g, 0, 0, 0, 0)),
        ],
        out_specs=[pl.BlockSpec(q_blk, lambda b, r, c: (b, r, c, 0)),
                   pl.BlockSpec(q_blk, lambda b, r, c: (b, r, c, 0))],
        compiler_params=_cparams(("parallel", "parallel", "arbitrary")),
        name=f"dilated_attn_g{g}",
    )(zg, zg, zg, bias)


def _dft_angle(r, c, n):
    return ((r * c) & (n - 1)).astype(F32) * (2.0 * math.pi / n)


def _dft_base_body(n, cos_ref, sin_ref):
    rows = cos_ref.shape[0]
    ang = _dft_angle(lax.broadcasted_iota(jnp.int32, (rows, n), 0),
                     lax.broadcasted_iota(jnp.int32, (rows, n), 1), n)
    cos_ref[...] = jnp.cos(ang)
    sin_ref[...] = jnp.sin(ang)


def _dft_table_body(n, neg_sin, cb_ref, sb_ref, o_ref):
    rows = o_ref.shape[0]
    ang0 = _dft_angle(pl.program_id(0) * rows, lax.broadcasted_iota(jnp.int32, (1, n), 1), n)
    c0, s0 = jnp.cos(ang0), jnp.sin(ang0)
    cb, sb = cb_ref[...], sb_ref[...]
    sn = s0 * cb + c0 * sb
    o_ref[:, :n] = (c0 * cb - s0 * sb).astype(o_ref.dtype)
    o_ref[:, n:] = (-sn if neg_sin else sn).astype(o_ref.dtype)


def _dft_table(n, neg_sin, rows):
    assert n & (n - 1) == 0
    base_sds = jax.ShapeDtypeStruct((rows, n), F32)
    cb, sb = pl.pallas_call(
        functools.partial(_dft_base_body, n),
        out_shape=[base_sds, base_sds],
        compiler_params=_cparams(()),
        name=f"dft_base_{n}",
    )()
    return pl.pallas_call(
        functools.partial(_dft_table_body, n, neg_sin),
        out_shape=jax.ShapeDtypeStruct((n, 2 * n), BF16),
        grid=(n // rows,),
        in_specs=[pl.BlockSpec((rows, n), lambda i: (0, 0))] * 2,
        out_specs=pl.BlockSpec((rows, 2 * n), lambda i: (i, 0)),
        compiler_params=_cparams(("parallel",)),
        name=f"dft_table_{n}",
    )(cb, sb)


def _fourier_body(S, rows1, ua_ref, ub_ref, wc_ref, ws_ref, o_ref, y_scr):
    C = FNET_GROUP_DIM

    @pl.when(pl.program_id(1) == 0)
    def _():
        for u_ref, base in ((ua_ref, 0), (ub_ref, COL_BLK)):
            for k in range(COL_BLK // C):
                for r0 in range(0, S, rows1):
                    y = jnp.dot(u_ref[0, r0:r0 + rows1, k * C:(k + 1) * C], wc_ref[...],
                                preferred_element_type=F32)
                    c0 = base + k * C
                    y_scr[r0:r0 + rows1, c0:c0 + C] = y[:, :C].astype(BF16)
                    y_scr[S + r0:S + r0 + rows1, c0:c0 + C] = y[:, C:].astype(BF16)

    scale = 1.0 / math.sqrt(S * C)
    f = jnp.dot(ws_ref[...], y_scr[...], preferred_element_type=F32) * scale
    o_ref[0] = f.astype(o_ref.dtype)


def _fourier(z3, wc, ws, *, tr=256, rows1=512):
    B, S, C = z3.shape
    cb = NAT_FNET_COL // COL_BLK
    return pl.pallas_call(
        functools.partial(_fourier_body, S, rows1),
        out_shape=jax.ShapeDtypeStruct((B, S, FNET_WIDTH), BF16),
        grid=(B, S // tr),
        in_specs=[
            pl.BlockSpec((1, S, COL_BLK), lambda b, i: (b, 0, cb)),
            pl.BlockSpec((1, S, COL_BLK), lambda b, i: (b, 0, cb + 1)),
            pl.BlockSpec((FNET_GROUP_DIM, 2 * FNET_GROUP_DIM), lambda b, i: (0, 0)),
            pl.BlockSpec((tr, 2 * S), lambda b, i: (i, 0)),
        ],
        out_specs=pl.BlockSpec((1, tr, FNET_WIDTH), lambda b, i: (b, i, 0)),
        scratch_shapes=[pltpu.VMEM((2 * S, FNET_WIDTH), BF16)],
        compiler_params=_cparams(("parallel", "arbitrary")),
        name="fourier",
    )(z3, z3, wc, ws).reshape(B * S, FNET_WIDTH)


def _memory_body(qa_ref, qb_ref, mem_ref, gm_ref, wkv_ref, o_ref, k_scr, v_scr):
    E = MEM_HEAD_DIM

    @pl.when(pl.program_id(1) == 0)
    def _():
        hm = _rms_f32(mem_ref[0], gm_ref[...]).astype(BF16)
        kv = jnp.dot(hm, wkv_ref[...], preferred_element_type=F32)
        k_scr[...] = (kv[:, :MEM_WIDTH] * (E ** -0.5)).astype(BF16)
        v_scr[...] = kv[:, MEM_WIDTH:].astype(BF16)

    for q_ref, base in ((qa_ref, 0), (qb_ref, COL_BLK)):
        for k in range(COL_BLK // E):
            c0 = base + k * E
            s = lax.dot_general(q_ref[0, :, k * E:(k + 1) * E], k_scr[:, c0:c0 + E],
                                (((1,), (1,)), ((), ())), preferred_element_type=F32)
            m = jnp.max(s, axis=-1, keepdims=True)
            p = jnp.exp(s - m)
            den = jnp.sum(p, axis=-1, keepdims=True)
            o = jnp.dot(p.astype(BF16), v_scr[:, c0:c0 + E], preferred_element_type=F32) / den
            o_ref[0, :, c0:c0 + E] = o.astype(o_ref.dtype)


def _memory(z3, mem, g_mem, w_kv, *, tq=1024):
    B, S, C = z3.shape
    M, D = mem.shape[1:]
    cb = NAT_MEM_COL // COL_BLK
    assert MEM_HEAD_DIM ** -0.5 == 2.0 ** round(math.log2(MEM_HEAD_DIM ** -0.5))
    return pl.pallas_call(
        _memory_body,
        out_shape=jax.ShapeDtypeStruct((B, S, MEM_WIDTH), BF16),
        grid=(B, S // tq),
        in_specs=[
            pl.BlockSpec((1, tq, COL_BLK), lambda b, i: (b, i, cb)),
            pl.BlockSpec((1, tq, COL_BLK), lambda b, i: (b, i, cb + 1)),
            pl.BlockSpec((1, M, D), lambda b, i: (b, 0, 0)),
            pl.BlockSpec((1, D), lambda b, i: (0, 0)),
            pl.BlockSpec((D, 2 * MEM_WIDTH), lambda b, i: (0, 0)),
        ],
        out_specs=pl.BlockSpec((1, tq, MEM_WIDTH), lambda b, i: (b, i, 0)),
        scratch_shapes=[pltpu.VMEM((M, MEM_WIDTH), BF16), pltpu.VMEM((M, MEM_WIDTH), BF16)],
        compiler_params=_cparams(("parallel", "arbitrary")),
        name="memory_attn",
    )(z3, z3, mem, g_mem, w_kv).reshape(B * S, MEM_WIDTH)


def _combine_body(o0_ref, o1_ref, o2_ref, l0_ref, l1_ref, l2_ref, a_ref, nat_scr):
    tm = a_ref.shape[0]
    nc = GROUP_WIDTH // LANES

    def to_natural(ref, slot):
        dil = ref.shape[1]
        for r in range(dil):
            for c in range(nc):
                nat_scr[slot * nc + c, pl.ds(r, tm // dil, stride=dil), :] = (
                    ref[0, r, :, c * LANES:(c + 1) * LANES])

    def natural_rows(ref, slot, rows):
        if ref.shape[1] == 1:
            return ref[0, 0, rows, :]
        return jnp.concatenate([nat_scr[slot * nc + c, rows, :] for c in range(nc)], axis=1)

    slots = {}
    for ref in (l1_ref, l2_ref, o1_ref, o2_ref):
        slots[id(ref)] = len(slots)
        to_natural(ref, slots[id(ref)])

    def combine(rows):
        l0 = natural_rows(l0_ref, None, rows)
        l1 = natural_rows(l1_ref, slots[id(l1_ref)], rows)
        l2 = natural_rows(l2_ref, slots[id(l2_ref)], rows)
        mx = jnp.maximum(jnp.maximum(l0, l1), l2)
        e0, e1, e2 = jnp.exp(l0 - mx), jnp.exp(l1 - mx), jnp.exp(l2 - mx)
        num = (e0 * natural_rows(o0_ref, None, rows)
               + e1 * natural_rows(o1_ref, slots[id(o1_ref)], rows)
               + e2 * natural_rows(o2_ref, slots[id(o2_ref)], rows))
        a_ref[rows, :] = (num / (e0 + e1 + e2)).astype(BF16)

    _for_row_chunks(tm, combine)


def _combine(seq, o_groups, lse_groups, *, tm=512):
    B = o_groups[0].shape[0]
    tps = seq // tm
    group_specs = [pl.BlockSpec((1, a.shape[1], tm // a.shape[1], GROUP_WIDTH),
                                lambda i: (i // tps, 0, i % tps, 0))
                   for a in (*o_groups, *lse_groups)]
    return pl.pallas_call(
        _combine_body,
        out_shape=jax.ShapeDtypeStruct((B * seq, GROUP_WIDTH), BF16),
        grid=(B * tps,),
        in_specs=group_specs,
        out_specs=pl.BlockSpec((tm, GROUP_WIDTH), lambda i: (i, 0)),
        scratch_shapes=[pltpu.VMEM((2 * (len(DILATED_GROUPS) - 1) * GROUP_WIDTH // LANES, tm, LANES), F32)],
        compiler_params=_cparams(("parallel",)),
        name="group_combine",
    )(*o_groups, *lse_groups)


def _gated_body(h_ref, a_ref, f_ref, m_ref, wg0_ref, wg1_ref, wg2_ref, bg0_ref, bg1_ref, bg2_ref,
                wpa_ref, wpf_ref, wpm_ref, out_ref):
    h = h_ref[...]

    def gate(w_ref, b_ref):
        return jax.nn.sigmoid(jnp.dot(h, w_ref[...], preferred_element_type=F32) + b_ref[...])

    mg = gate(wg0_ref, bg0_ref) * jnp.dot(a_ref[...], wpa_ref[...], preferred_element_type=F32)
    mg += gate(wg1_ref, bg1_ref) * jnp.dot(f_ref[...], wpf_ref[...], preferred_element_type=F32)
    mg += gate(wg2_ref, bg2_ref) * jnp.dot(m_ref[...], wpm_ref[...], preferred_element_type=F32)
    out_ref[...] = mg.astype(out_ref.dtype)


def _gated_merge(h, a, f, m, w_gate, b_gate, w_pa, w_pf, w_pm, *, tm=1024, tn=256):
    T, D = h.shape
    nj = D // tn
    row = lambda i, j: (i, 0)
    colj = lambda i, j: (0, j)
    gate_specs = [pl.BlockSpec((D, tn), lambda i, j, k=k: (0, j + k * nj)) for k in range(3)]
    bias_specs = [pl.BlockSpec((1, tn), lambda i, j, k=k: (0, j + k * nj)) for k in range(3)]
    return pl.pallas_call(
        _gated_body,
        out_shape=jax.ShapeDtypeStruct((T, D), BF16),
        grid=(T // tm, nj),
        in_specs=[pl.BlockSpec((tm, D), row), pl.BlockSpec((tm, GROUP_WIDTH), row),
                  pl.BlockSpec((tm, FNET_WIDTH), row), pl.BlockSpec((tm, MEM_WIDTH), row)]
                 + gate_specs + bias_specs
                 + [pl.BlockSpec((GROUP_WIDTH, tn), colj), pl.BlockSpec((FNET_WIDTH, tn), colj),
                    pl.BlockSpec((MEM_WIDTH, tn), colj)],
        out_specs=pl.BlockSpec((tm, tn), lambda i, j: (i, j)),
        compiler_params=_cparams(("parallel", "parallel")),
        name="gated_merge",
    )(h, a, f, m, w_gate, w_gate, w_gate, b_gate, b_gate, b_gate, w_pa, w_pf, w_pm)


def _out_proj_body(mg_ref, x_ref, wo_ref, gpost_ref, out_ref, y_scr):
    y_scr[...] = jnp.dot(mg_ref[...], wo_ref[...], preferred_element_type=F32)
    gpost = gpost_ref[...]

    def post(rows):
        out_ref[rows, :] = x_ref[rows, :] + _rms_f32(y_scr[rows, :], gpost)

    _for_row_chunks(x_ref.shape[0], post)


def _out_proj(mg, x, w_out, g_post, *, tm=512):
    T, D = x.shape
    row = lambda i: (i, 0)
    const = lambda i: (0, 0)
    return pl.pallas_call(
        _out_proj_body,
        out_shape=jax.ShapeDtypeStruct((T, D), F32),
        grid=(T // tm,),
        in_specs=[pl.BlockSpec((tm, D), row), pl.BlockSpec((tm, D), row),
                  pl.BlockSpec((D, D), const), pl.BlockSpec((1, D), const)],
        out_specs=pl.BlockSpec((tm, D), row),
        scratch_shapes=[pltpu.VMEM((tm, D), F32)],
        compiler_params=_cparams(("parallel",)),
        name="out_proj",
    )(mg, x, w_out, g_post)


def _layer_params(l, ffn1_norm_pre, ffn1_w_in, ffn1_w_out, ffn1_norm_post, mix_norm_pre, mem_norm, w_in,
                  w_mem_kv, w_gate, b_gate, w_proj_attn, w_proj_fnet, w_proj_mem, w_out, mix_norm_post,
                  ffn2_norm_pre, ffn2_w_in, ffn2_w_out, ffn2_norm_post):
    vec = lambda a: a[l][None, :]
    mat = lambda a: a[l].astype(BF16)
    wi = mat(w_in)
    qkv = lambda g: [wi[:, s * ATTN_WIDTH + g * GROUP_WIDTH: s * ATTN_WIDTH + (g + 1) * GROUP_WIDTH]
                     for s in range(3)]
    w_nat = jnp.concatenate(qkv(0) + [wi[:, 3 * ATTN_WIDTH:]], axis=1)
    w_groups = [jnp.concatenate(qkv(g), axis=1) for g in range(1, len(DILATED_GROUPS))]
    return dict(f1_pre=vec(ffn1_norm_pre), f1_wi=mat(ffn1_w_in), f1_wo=mat(ffn1_w_out),
                f1_post=vec(ffn1_norm_post), mix_pre=vec(mix_norm_pre), mem_norm=vec(mem_norm),
                w_nat=w_nat, w_groups=w_groups, w_mem_kv=mat(w_mem_kv), w_gate=mat(w_gate),
                b_gate=vec(b_gate), w_pa=mat(w_proj_attn), w_pf=mat(w_proj_fnet),
                w_pm=mat(w_proj_mem), w_out=mat(w_out), mix_post=vec(mix_norm_post),
                f2_pre=vec(ffn2_norm_pre), f2_wi=mat(ffn2_w_in), f2_wo=mat(ffn2_w_out),
                f2_post=vec(ffn2_norm_post))


def _layer(x, mem, bias, wc, ws, p):
    B, S, D = x.shape
    dils = [dil for _, dil in DILATED_GROUPS]
    assert dils[0] == 1
    xt = x.reshape(B * S, D)
    x1, hs = _ffn(xt, S, p["f1_pre"], p["f1_wi"], p["f1_wo"], p["f1_post"], p["mix_pre"], dils)
    h = hs[0].reshape(B * S, D)
    z0 = _in_proj(h, p["w_nat"])
    zs = [z0] + [_in_proj(hd.reshape(B * S, D), w) for hd, w in zip(hs[1:], p["w_groups"])]
    o_groups, lse_groups = [], []
    for g, (dil, z) in enumerate(zip(dils, zs)):
        o, lse = _dilated_group(z.reshape(B, dil, S // dil, z.shape[-1]), bias, g)
        o_groups.append(o)
        lse_groups.append(lse)
    z3 = z0.reshape(B, S, -1)
    f = _fourier(z3, wc, ws)
    m = _memory(z3, mem, p["mem_norm"], p["w_mem_kv"])
    a = _combine(S, o_groups, lse_groups)
    mg = _gated_merge(h, a, f, m, p["w_gate"], p["b_gate"], p["w_pa"], p["w_pf"], p["w_pm"])
    x2 = _out_proj(mg, x1, p["w_out"], p["mix_post"])
    x3, _ = _ffn(x2, S, p["f2_pre"], p["f2_wi"], p["f2_wo"], p["f2_post"], p["f2_post"], (), tm=1024)
    return x3.reshape(B, S, D)


def kernel(x_prompt, x_sample, mem_prompt, mem_sample, rel_bias, ffn1_norm_pre, ffn1_w_in, ffn1_w_out, ffn1_norm_post, mix_norm_pre, mem_norm, w_in, w_mem_kv, w_gate, b_gate, w_proj_attn, w_proj_fnet, w_proj_mem, w_out, mix_norm_post, ffn2_norm_pre, ffn2_w_in, ffn2_w_out, ffn2_norm_post):
    depth = w_in.shape[0]
    bias = _bias_tables(rel_bias)
    wc = _dft_table(FNET_GROUP_DIM, False, FNET_GROUP_DIM)
    ws = {s: _dft_table(s, True, 256) for s in {x_prompt.shape[1], x_sample.shape[1]}}
    y_prompt, y_sample = x_prompt, x_sample
    for l in range(depth):
        p = _layer_params(l, ffn1_norm_pre, ffn1_w_in, ffn1_w_out, ffn1_norm_post, mix_norm_pre, mem_norm,
                          w_in, w_mem_kv, w_gate, b_gate, w_proj_attn, w_proj_fnet, w_proj_mem, w_out,
                          mix_norm_post, ffn2_norm_pre, ffn2_w_in, ffn2_w_out, ffn2_norm_post)
        y_prompt = _layer(y_prompt, mem_prompt, bias, wc, ws[y_prompt.shape[1]], p)
        y_sample = _layer(y_sample, mem_sample, bias, wc, ws[y_sample.shape[1]], p)
    return (y_prompt, y_sample)
```

```python
import functools
import math

import jax
import jax.numpy as jnp
from jax import lax
from jax.experimental import pallas as pl
from jax.experimental.pallas import tpu as pltpu

F32 = jnp.float32
BF16 = jnp.bfloat16

RMS_EPS = 1e-6
NEG_INF = -1e30

DILATED_GROUPS = ((128, 1), (512, 4), (2048, 16))
GROUP_HEADS = 4
HEAD_DIM = 128
GROUP_WIDTH = GROUP_HEADS * HEAD_DIM
ATTN_WIDTH = len(DILATED_GROUPS) * GROUP_WIDTH
FNET_GROUPS = 4
FNET_GROUP_DIM = 256
FNET_WIDTH = FNET_GROUPS * FNET_GROUP_DIM
MEM_HEADS = 4
MEM_HEAD_DIM = 256
MEM_WIDTH = MEM_HEADS * MEM_HEAD_DIM
N_BUCKETS = 32
MAX_EXACT = 8
REL_MAX_DIST = 1024

Q_BLK = 128
HALF_WIN = 64
KEY_WIN = 256
COL_BLK = 512
NAT_FNET_COL = 3 * GROUP_WIDTH
NAT_MEM_COL = NAT_FNET_COL + FNET_WIDTH

V7X_VMEM_LIMIT_BYTES = 58 * 1024 * 1024
LANES = 128


def _cparams(semantics, vmem_bytes=V7X_VMEM_LIMIT_BYTES):
    return pltpu.CompilerParams(dimension_semantics=semantics, vmem_limit_bytes=vmem_bytes)


def _rms_f32(x, g):
    return x * lax.rsqrt(jnp.mean(x * x, axis=-1, keepdims=True) + RMS_EPS) * g


ROW_CHUNK = 16


def _for_row_chunks(nrows, fn):
    for k in range(nrows // ROW_CHUNK):
        fn(slice(k * ROW_CHUNK, (k + 1) * ROW_CHUNK))


def _ffn_body(x_ref, gpre_ref, wia_ref, wib_ref, wo_ref, gpost_ref, out_ref, h_scr):
    tm = x_ref.shape[0]
    j = pl.program_id(1)
    nj = pl.num_programs(1)

    def swiglu_out():
        h = h_scr[...]
        a = jnp.dot(h, wia_ref[...], preferred_element_type=F32)
        b = jnp.dot(h, wib_ref[...], preferred_element_type=F32)
        g = (a * jax.nn.sigmoid(a) * b).astype(BF16)
        return jnp.dot(g, wo_ref[...], preferred_element_type=F32)

    @pl.when(j == 0)
    def _():
        gpre = gpre_ref[...]

        def pre(rows):
            h_scr[rows, :] = _rms_f32(x_ref[rows, :], gpre).astype(BF16)

        _for_row_chunks(tm, pre)
        out_ref[...] = swiglu_out()

    @pl.when(j > 0)
    def _():
        out_ref[...] += swiglu_out()

    @pl.when(j == nj - 1)
    def _():
        gpost = 0.5 * gpost_ref[...]

        def post(rows):
            out_ref[rows, :] = x_ref[rows, :] + _rms_f32(out_ref[rows, :], gpost)

        _for_row_chunks(tm, post)


def _ffn(x, g_pre, w_in, w_out, g_post, *, tm=1024, tf=512):
    T, D = x.shape
    F = w_out.shape[0]
    nj = F // tf
    row = lambda i, j: (i, 0)
    const = lambda i, j: (0, 0)
    return pl.pallas_call(
        _ffn_body,
        out_shape=jax.ShapeDtypeStruct((T, D), F32),
        grid=(T // tm, nj),
        in_specs=[
            pl.BlockSpec((tm, D), row),
            pl.BlockSpec((1, D), const),
            pl.BlockSpec((D, tf), lambda i, j: (0, j)),
            pl.BlockSpec((D, tf), lambda i, j: (0, j + nj)),
            pl.BlockSpec((tf, D), lambda i, j: (j, 0)),
            pl.BlockSpec((1, D), const),
        ],
        out_specs=pl.BlockSpec((tm, D), row),
        scratch_shapes=[pltpu.VMEM((tm, D), BF16)],
        compiler_params=_cparams(("parallel", "arbitrary")),
        name="ffn",
    )(x, g_pre, w_in, w_in, w_out, g_post)


def _norm_regroup_body(dils, x_ref, g_ref, *rest):
    hn_refs = rest[:len(dils)]
    scratch = rest[len(dils):]
    tm = x_ref.shape[0]
    g = g_ref[...]

    def norm(rows):
        hn = _rms_f32(x_ref[rows, :], g)
        for hn_ref, dil in zip(hn_refs, dils):
            if dil == 1:
                hn_ref[0, 0, rows, :] = hn.astype(BF16)
        if scratch:
            for c in range(scratch[0].shape[0]):
                scratch[0][c, rows, :] = hn[:, c * LANES:(c + 1) * LANES]

    _for_row_chunks(tm, norm)
    prev_scr, prev_dil, level = (scratch[0] if scratch else None), 1, 1
    for hn_ref, dil in zip(hn_refs, dils):
        if dil == 1:
            continue
        ratio, n = dil // prev_dil, tm // dil
        cur_scr = scratch[level] if level < len(scratch) else None
        for rp in range(prev_dil):
            for m in range(ratio):
                r = rp + prev_dil * m
                for c in range(prev_scr.shape[0]):
                    piece = prev_scr[c, pl.ds(rp * (tm // prev_dil) + m, n, stride=ratio), :]
                    hn_ref[0, r, :, c * LANES:(c + 1) * LANES] = piece.astype(BF16)
                    if cur_scr is not None:
                        cur_scr[c, r * n:(r + 1) * n, :] = piece
        prev_scr, prev_dil, level = cur_scr, dil, level + 1


def _norm_regroup(x, seq, g, dils, *, tm=512):
    T, D = x.shape
    tps = seq // tm
    strided = [dil for dil in dils if dil > 1]
    assert all(b % a == 0 for a, b in zip([1] + strided, strided))
    return pl.pallas_call(
        functools.partial(_norm_regroup_body, tuple(dils)),
        out_shape=[jax.ShapeDtypeStruct((T // seq, dil, seq // dil, D), BF16) for dil in dils],
        grid=(T // tm,),
        in_specs=[pl.BlockSpec((tm, D), lambda i: (i, 0)), pl.BlockSpec((1, D), lambda i: (0, 0))],
        out_specs=[pl.BlockSpec((1, dil, tm // dil, D), lambda i: (i // tps, 0, i % tps, 0))
                   for dil in dils],
        scratch_shapes=[pltpu.VMEM((D // LANES, tm, LANES), F32)] * len(strided),
        compiler_params=_cparams(("parallel",)),
        name="norm_regroup",
    )(x, g)


def _matmul_body(h_ref, w_ref, o_ref):
    o_ref[...] = jnp.dot(h_ref[...], w_ref[...], preferred_element_type=F32).astype(o_ref.dtype)


def _in_proj(h, w, *, tm=1024, tn_max=1792):
    T, D = h.shape
    N = w.shape[1]
    tn = min(N, tn_max)
    return pl.pallas_call(
        _matmul_body,
        out_shape=jax.ShapeDtypeStruct((T, N), BF16),
        grid=(N // tn, T // tm),
        in_specs=[pl.BlockSpec((tm, D), lambda n, i: (i, 0)),
                  pl.BlockSpec((D, tn), lambda n, i: (0, n))],
        out_specs=pl.BlockSpec((tm, tn), lambda n, i: (i, n)),
        compiler_params=_cparams(("parallel", "parallel")),
        name="in_proj",
    )(h, w)


def _t5_bucket(rel):
    half = N_BUCKETS // 2
    n = jnp.abs(rel)
    nf = jnp.maximum(n, 1).astype(F32)
    large = MAX_EXACT + (jnp.log(nf / MAX_EXACT) / math.log(REL_MAX_DIST / MAX_EXACT)
                         * (half - MAX_EXACT)).astype(jnp.int32)
    large = jnp.minimum(large, half - 1)
    return jnp.where(rel > 0, half, 0) + jnp.where(n < MAX_EXACT, n, large)


WIN_OFFSETS = (-HALF_WIN, 0, Q_BLK - KEY_WIN)


def _bias_body(tbl_ref, out_ref):
    g = pl.program_id(0)
    w = pl.program_id(1)
    dil = jnp.where(g == 0, DILATED_GROUPS[0][1],
                    jnp.where(g == 1, DILATED_GROUPS[1][1], DILATED_GROUPS[2][1]))
    off = jnp.where(w == 0, WIN_OFFSETS[0], jnp.where(w == 1, WIN_OFFSETS[1], WIN_OFFSETS[2]))
    s = lax.broadcasted_iota(jnp.int32, (Q_BLK, KEY_WIN), 0)
    t = lax.broadcasted_iota(jnp.int32, (Q_BLK, KEY_WIN), 1)
    delta = t - s + off
    bucket = _t5_bucket(delta * dil)
    valid = jnp.abs(delta) <= HALF_WIN
    for h in range(GROUP_HEADS):
        col = g * GROUP_HEADS + h

        def pick(i, acc):
            return jnp.where(bucket == i, tbl_ref[i, col], acc)

        bias = lax.fori_loop(0, N_BUCKETS, pick, jnp.zeros((Q_BLK, KEY_WIN), F32))
        out_ref[0, 0, h] = jnp.where(valid, bias, NEG_INF)


def _bias_tables(rel_bias):
    ng = len(DILATED_GROUPS)
    return pl.pallas_call(
        _bias_body,
        out_shape=jax.ShapeDtypeStruct((ng, len(WIN_OFFSETS), GROUP_HEADS, Q_BLK, KEY_WIN), F32),
        grid=(ng, len(WIN_OFFSETS)),
        in_specs=[pl.BlockSpec(memory_space=pltpu.SMEM)],
        out_specs=pl.BlockSpec((1, 1, GROUP_HEADS, Q_BLK, KEY_WIN), lambda g, w: (g, w, 0, 0, 0)),
        compiler_params=_cparams(("arbitrary", "arbitrary")),
        name="rel_bias",
    )(rel_bias)


def _attn_body(L, TL, q_ref, k_ref, v_ref, bias_ref, o_ref, lse_ref):
    scale = HEAD_DIM ** -0.5
    nblk = L // Q_BLK
    per_res = TL // Q_BLK
    n0 = pl.program_id(2) * per_res
    kv_rows = k_ref.shape[1]
    ones = jnp.ones((KEY_WIN, HEAD_DIM), BF16)
    key_half = lax.broadcasted_iota(jnp.int32, (Q_BLK, KEY_WIN), 1) // Q_BLK

    def block(i, carry):
        rb = i // per_res
        n = n0 + i % per_res
        qs = pl.multiple_of(i * Q_BLK, Q_BLK)
        if L >= KEY_WIN:
            ks = rb * L + jnp.clip(n * Q_BLK - HALF_WIN, 0, L - KEY_WIN)
            kind = jnp.where(n == 0, 1, jnp.where(n == nblk - 1, 2, 0))
        else:
            assert L == Q_BLK and kv_rows >= KEY_WIN
            ks = jnp.minimum(rb * L, kv_rows - KEY_WIN)
            own_half = (rb * L - ks) // Q_BLK
            kind = 1 + own_half
        ks = pl.multiple_of(ks, HALF_WIN)
        for h in range(GROUP_HEADS):
            cols = slice(h * HEAD_DIM, (h + 1) * HEAD_DIM)
            qh = q_ref[0, pl.ds(qs, Q_BLK), cols]
            kh = k_ref[0, pl.ds(ks, KEY_WIN), cols]
            vh = v_ref[0, pl.ds(ks, KEY_WIN), cols]
            s = lax.dot_general(qh, kh, (((1,), (1,)), ((), ())), preferred_element_type=F32)
            s = s * scale + bias_ref[0, kind, h]
            if L < KEY_WIN:
                s = jnp.where(key_half == own_half, s, NEG_INF)
            m = jnp.max(s, axis=-1, keepdims=True)
            p = jnp.exp(s - m).astype(BF16)
            od = jnp.dot(p, jnp.concatenate([vh, ones], axis=1), preferred_element_type=F32)
            den = od[:, HEAD_DIM:]
            o_ref[0, pl.ds(qs, Q_BLK), cols] = od[:, :HEAD_DIM] / den
            lse_ref[0, pl.ds(qs, Q_BLK), cols] = m + jnp.log(den)
        return carry

    lax.fori_loop(0, q_ref.shape[1] // Q_BLK, block, 0, unroll=4)


ATTN_ROWS_PER_STEP = 1024


def _dilated_group(zg, bias, g):
    B, dil, L, C = zg.shape
    TL = min(ATTN_ROWS_PER_STEP, L)
    RB = min(dil, ATTN_ROWS_PER_STEP // TL)
    z3 = zg.reshape(B, dil * L, C)
    kv_blk = (1, RB * L, GROUP_WIDTH)
    q_blk = (1, RB * TL, GROUP_WIDTH)
    q_idx = lambda b, r, c: (b, r * (L // TL) + c, 0)
    out_sds = jax.ShapeDtypeStruct((B, dil * L, GROUP_WIDTH), F32)
    o, lse = pl.pallas_call(
        functools.partial(_attn_body, L, TL),
        out_shape=[out_sds, out_sds],
        grid=(B, dil // RB, L // TL),
        in_specs=[
            pl.BlockSpec(q_blk, q_idx),
            pl.BlockSpec(kv_blk, lambda b, r, c: (b, r, 1)),
            pl.BlockSpec(kv_blk, lambda b, r, c: (b, r, 2)),
            pl.BlockSpec((1, len(WIN_OFFSETS), GROUP_HEADS, Q_BLK, KEY_WIN),
                         lambda b, r, c: (g, 0, 0, 0, 0)),
        ],
        out_specs=[pl.BlockSpec(q_blk, q_idx), pl.BlockSpec(q_blk, q_idx)],
        compiler_params=_cparams(("parallel", "parallel", "arbitrary")),
        name=f"dilated_attn_g{g}",
    )(z3, z3, z3, bias)
    return o.reshape(B, dil, L, GROUP_WIDTH), lse.reshape(B, dil, L, GROUP_WIDTH)


def _dft_angle(r, c, n):
    return ((r * c) & (n - 1)).astype(F32) * (2.0 * math.pi / n)


def _dft_base_body(n, cos_ref, sin_ref):
    rows = cos_ref.shape[0]
    ang = _dft_angle(lax.broadcasted_iota(jnp.int32, (rows, n), 0),
                     lax.broadcasted_iota(jnp.int32, (rows, n), 1), n)
    cos_ref[...] = jnp.cos(ang)
    sin_ref[...] = jnp.sin(ang)


def _dft_table_body(n, neg_sin, cb_ref, sb_ref, o_ref):
    rows = o_ref.shape[0]
    ang0 = _dft_angle(pl.program_id(0) * rows, lax.broadcasted_iota(jnp.int32, (1, n), 1), n)
    c0, s0 = jnp.cos(ang0), jnp.sin(ang0)
    cb, sb = cb_ref[...], sb_ref[...]
    sn = s0 * cb + c0 * sb
    o_ref[:, :n] = (c0 * cb - s0 * sb).astype(o_ref.dtype)
    o_ref[:, n:] = (-sn if neg_sin else sn).astype(o_ref.dtype)


def _dft_table(n, neg_sin, rows):
    assert n & (n - 1) == 0
    base_sds = jax.ShapeDtypeStruct((rows, n), F32)
    cb, sb = pl.pallas_call(
        functools.partial(_dft_base_body, n),
        out_shape=[base_sds, base_sds],
        compiler_params=_cparams(()),
        name=f"dft_base_{n}",
    )()
    return pl.pallas_call(
        functools.partial(_dft_table_body, n, neg_sin),
        out_shape=jax.ShapeDtypeStruct((n, 2 * n), BF16),
        grid=(n // rows,),
        in_specs=[pl.BlockSpec((rows, n), lambda i: (0, 0))] * 2,
        out_specs=pl.BlockSpec((rows, 2 * n), lambda i: (i, 0)),
        compiler_params=_cparams(("parallel",)),
        name=f"dft_table_{n}",
    )(cb, sb)


def _fourier_body(S, rows1, ua_ref, ub_ref, wc_ref, ws_ref, o_ref, y_scr):
    C = FNET_GROUP_DIM

    @pl.when(pl.program_id(1) == 0)
    def _():
        for u_ref, base in ((ua_ref, 0), (ub_ref, COL_BLK)):
            for k in range(COL_BLK // C):
                for r0 in range(0, S, rows1):
                    y = jnp.dot(u_ref[0, r0:r0 + rows1, k * C:(k + 1) * C], wc_ref[...],
                                preferred_element_type=F32)
                    c0 = base + k * C
                    y_scr[r0:r0 + rows1, c0:c0 + C] = y[:, :C].astype(BF16)
                    y_scr[S + r0:S + r0 + rows1, c0:c0 + C] = y[:, C:].astype(BF16)

    scale = 1.0 / math.sqrt(S * C)
    f = jnp.dot(ws_ref[...], y_scr[...], preferred_element_type=F32) * scale
    o_ref[0] = f.astype(o_ref.dtype)


def _fourier(z3, wc, ws, *, tr=256, rows1=512):
    B, S, C = z3.shape
    cb = NAT_FNET_COL // COL_BLK
    return pl.pallas_call(
        functools.partial(_fourier_body, S, rows1),
        out_shape=jax.ShapeDtypeStruct((B, S, FNET_WIDTH), BF16),
        grid=(B, S // tr),
        in_specs=[
            pl.BlockSpec((1, S, COL_BLK), lambda b, i: (b, 0, cb)),
            pl.BlockSpec((1, S, COL_BLK), lambda b, i: (b, 0, cb + 1)),
            pl.BlockSpec((FNET_GROUP_DIM, 2 * FNET_GROUP_DIM), lambda b, i: (0, 0)),
            pl.BlockSpec((tr, 2 * S), lambda b, i: (i, 0)),
        ],
        out_specs=pl.BlockSpec((1, tr, FNET_WIDTH), lambda b, i: (b, i, 0)),
        scratch_shapes=[pltpu.VMEM((2 * S, FNET_WIDTH), BF16)],
        compiler_params=_cparams(("parallel", "arbitrary")),
        name="fourier",
    )(z3, z3, wc, ws).reshape(B * S, FNET_WIDTH)


def _memory_body(qa_ref, qb_ref, mem_ref, gm_ref, wkv_ref, o_ref, k_scr, v_scr):
    E = MEM_HEAD_DIM

    @pl.when(pl.program_id(1) == 0)
    def _():
        hm = _rms_f32(mem_ref[0], gm_ref[...]).astype(BF16)
        kv = jnp.dot(hm, wkv_ref[...], preferred_element_type=F32)
        k_scr[...] = (kv[:, :MEM_WIDTH] * (E ** -0.5)).astype(BF16)
        v_scr[...] = kv[:, MEM_WIDTH:].astype(BF16)

    for q_ref, base in ((qa_ref, 0), (qb_ref, COL_BLK)):
        for k in range(COL_BLK // E):
            c0 = base + k * E
            s = lax.dot_general(q_ref[0, :, k * E:(k + 1) * E], k_scr[:, c0:c0 + E],
                                (((1,), (1,)), ((), ())), preferred_element_type=F32)
            m = jnp.max(s, axis=-1, keepdims=True)
            p = jnp.exp(s - m)
            den = jnp.sum(p, axis=-1, keepdims=True)
            o = jnp.dot(p.astype(BF16), v_scr[:, c0:c0 + E], preferred_element_type=F32) / den
            o_ref[0, :, c0:c0 + E] = o.astype(o_ref.dtype)


def _memory(z3, mem, g_mem, w_kv, *, tq=1024):
    B, S, C = z3.shape
    M, D = mem.shape[1:]
    cb = NAT_MEM_COL // COL_BLK
    assert MEM_HEAD_DIM ** -0.5 == 2.0 ** round(math.log2(MEM_HEAD_DIM ** -0.5))
    return pl.pallas_call(
        _memory_body,
        out_shape=jax.ShapeDtypeStruct((B, S, MEM_WIDTH), BF16),
        grid=(B, S // tq),
        in_specs=[
            pl.BlockSpec((1, tq, COL_BLK), lambda b, i: (b, i, cb)),
            pl.BlockSpec((1, tq, COL_BLK), lambda b, i: (b, i, cb + 1)),
            pl.BlockSpec((1, M, D), lambda b, i: (b, 0, 0)),
            pl.BlockSpec((1, D), lambda b, i: (0, 0)),
            pl.BlockSpec((D, 2 * MEM_WIDTH), lambda b, i: (0, 0)),
        ],
        out_specs=pl.BlockSpec((1, tq, MEM_WIDTH), lambda b, i: (b, i, 0)),
        scratch_shapes=[pltpu.VMEM((M, MEM_WIDTH), BF16), pltpu.VMEM((M, MEM_WIDTH), BF16)],
        compiler_params=_cparams(("parallel", "arbitrary")),
        name="memory_attn",
    )(z3, z3, mem, g_mem, w_kv).reshape(B * S, MEM_WIDTH)


def _combine_body(o0_ref, o1_ref, o2_ref, l0_ref, l1_ref, l2_ref, a_ref, nat_scr):
    tm = a_ref.shape[0]
    nc = GROUP_WIDTH // LANES

    def to_natural(ref, slot):
        dil = ref.shape[1]
        for r in range(dil):
            for c in range(nc):
                nat_scr[slot * nc + c, pl.ds(r, tm // dil, stride=dil), :] = (
                    ref[0, r, :, c * LANES:(c + 1) * LANES])

    def natural_rows(ref, slot, rows):
        if ref.shape[1] == 1:
            return ref[0, 0, rows, :]
        return jnp.concatenate([nat_scr[slot * nc + c, rows, :] for c in range(nc)], axis=1)

    slots = {}
    for ref in (l1_ref, l2_ref, o1_ref, o2_ref):
        slots[id(ref)] = len(slots)
        to_natural(ref, slots[id(ref)])

    def combine(rows):
        l0 = natural_rows(l0_ref, None, rows)
        l1 = natural_rows(l1_ref, slots[id(l1_ref)], rows)
        l2 = natural_rows(l2_ref, slots[id(l2_ref)], rows)
        mx = jnp.maximum(jnp.maximum(l0, l1), l2)
        e0, e1, e2 = jnp.exp(l0 - mx), jnp.exp(l1 - mx), jnp.exp(l2 - mx)
        num = (e0 * natural_rows(o0_ref, None, rows)
               + e1 * natural_rows(o1_ref, slots[id(o1_ref)], rows)
               + e2 * natural_rows(o2_ref, slots[id(o2_ref)], rows))
        a_ref[rows, :] = (num / (e0 + e1 + e2)).astype(BF16)

    _for_row_chunks(tm, combine)


def _combine(seq, o_groups, lse_groups, *, tm=512):
    B = o_groups[0].shape[0]
    tps = seq // tm
    group_specs = [pl.BlockSpec((1, a.shape[1], tm // a.shape[1], GROUP_WIDTH),
                                lambda i: (i // tps, 0, i % tps, 0))
                   for a in (*o_groups, *lse_groups)]
    return pl.pallas_call(
        _combine_body,
        out_shape=jax.ShapeDtypeStruct((B * seq, GROUP_WIDTH), BF16),
        grid=(B * tps,),
        in_specs=group_specs,
        out_specs=pl.BlockSpec((tm, GROUP_WIDTH), lambda i: (i, 0)),
        scratch_shapes=[pltpu.VMEM((2 * (len(DILATED_GROUPS) - 1) * GROUP_WIDTH // LANES, tm, LANES), F32)],
        compiler_params=_cparams(("parallel",)),
        name="group_combine",
    )(*o_groups, *lse_groups)


def _gated_body(h_ref, a_ref, f_ref, m_ref, wg0_ref, wg1_ref, wg2_ref, bg0_ref, bg1_ref, bg2_ref,
                wpa_ref, wpf_ref, wpm_ref, out_ref):
    h = h_ref[...]

    def gate(w_ref, b_ref):
        return jax.nn.sigmoid(jnp.dot(h, w_ref[...], preferred_element_type=F32) + b_ref[...])

    mg = gate(wg0_ref, bg0_ref) * jnp.dot(a_ref[...], wpa_ref[...], preferred_element_type=F32)
    mg += gate(wg1_ref, bg1_ref) * jnp.dot(f_ref[...], wpf_ref[...], preferred_element_type=F32)
    mg += gate(wg2_ref, bg2_ref) * jnp.dot(m_ref[...], wpm_ref[...], preferred_element_type=F32)
    out_ref[...] = mg.astype(out_ref.dtype)


def _gated_merge(h, a, f, m, w_gate, b_gate, w_pa, w_pf, w_pm, *, tm=1024, tn=256):
    T, D = h.shape
    nj = D // tn
    row = lambda i, j: (i, 0)
    colj = lambda i, j: (0, j)
    gate_specs = [pl.BlockSpec((D, tn), lambda i, j, k=k: (0, j + k * nj)) for k in range(3)]
    bias_specs = [pl.BlockSpec((1, tn), lambda i, j, k=k: (0, j + k * nj)) for k in range(3)]
    return pl.pallas_call(
        _gated_body,
        out_shape=jax.ShapeDtypeStruct((T, D), BF16),
        grid=(T // tm, nj),
        in_specs=[pl.BlockSpec((tm, D), row), pl.BlockSpec((tm, GROUP_WIDTH), row),
                  pl.BlockSpec((tm, FNET_WIDTH), row), pl.BlockSpec((tm, MEM_WIDTH), row)]
                 + gate_specs + bias_specs
                 + [pl.BlockSpec((GROUP_WIDTH, tn), colj), pl.BlockSpec((FNET_WIDTH, tn), colj),
                    pl.BlockSpec((MEM_WIDTH, tn), colj)],
        out_specs=pl.BlockSpec((tm, tn), lambda i, j: (i, j)),
        compiler_params=_cparams(("parallel", "parallel")),
        name="gated_merge",
    )(h, a, f, m, w_gate, w_gate, w_gate, b_gate, b_gate, b_gate, w_pa, w_pf, w_pm)


def _out_proj_body(mg_ref, x_ref, wo_ref, gpost_ref, out_ref, y_scr):
    y_scr[...] = jnp.dot(mg_ref[...], wo_ref[...], preferred_element_type=F32)
    gpost = gpost_ref[...]

    def post(rows):
        out_ref[rows, :] = x_ref[rows, :] + _rms_f32(y_scr[rows, :], gpost)

    _for_row_chunks(x_ref.shape[0], post)


def _out_proj(mg, x, w_out, g_post, *, tm=512):
    T, D = x.shape
    row = lambda i: (i, 0)
    const = lambda i: (0, 0)
    return pl.pallas_call(
        _out_proj_body,
        out_shape=jax.ShapeDtypeStruct((T, D), F32),
        grid=(T // tm,),
        in_specs=[pl.BlockSpec((tm, D), row), pl.BlockSpec((tm, D), row),
                  pl.BlockSpec((D, D), const), pl.BlockSpec((1, D), const)],
        out_specs=pl.BlockSpec((tm, D), row),
        scratch_shapes=[pltpu.VMEM((tm, D), F32)],
        compiler_params=_cparams(("parallel",)),
        name="out_proj",
    )(mg, x, w_out, g_post)


def _layer_params(l, ffn1_norm_pre, ffn1_w_in, ffn1_w_out, ffn1_norm_post, mix_norm_pre, mem_norm, w_in,
                  w_mem_kv, w_gate, b_gate, w_proj_attn, w_proj_fnet, w_proj_mem, w_out, mix_norm_post,
                  ffn2_norm_pre, ffn2_w_in, ffn2_w_out, ffn2_norm_post):
    vec = lambda a: a[l][None, :]
    mat = lambda a: a[l].astype(BF16)
    wi = mat(w_in)
    qkv = lambda g: [wi[:, s * ATTN_WIDTH + g * GROUP_WIDTH: s * ATTN_WIDTH + (g + 1) * GROUP_WIDTH]
                     for s in range(3)]
    w_nat = jnp.concatenate(qkv(0) + [wi[:, 3 * ATTN_WIDTH:]], axis=1)
    w_groups = [jnp.concatenate(qkv(g), axis=1) for g in range(1, len(DILATED_GROUPS))]
    return dict(f1_pre=vec(ffn1_norm_pre), f1_wi=mat(ffn1_w_in), f1_wo=mat(ffn1_w_out),
                f1_post=vec(ffn1_norm_post), mix_pre=vec(mix_norm_pre), mem_norm=vec(mem_norm),
                w_nat=w_nat, w_groups=w_groups, w_mem_kv=mat(w_mem_kv), w_gate=mat(w_gate),
                b_gate=vec(b_gate), w_pa=mat(w_proj_attn), w_pf=mat(w_proj_fnet),
                w_pm=mat(w_proj_mem), w_out=mat(w_out), mix_post=vec(mix_norm_post),
                f2_pre=vec(ffn2_norm_pre), f2_wi=mat(ffn2_w_in), f2_wo=mat(ffn2_w_out),
                f2_post=vec(ffn2_norm_post))


def _layer(x, mem, bias, wc, ws, p):
    B, S, D = x.shape
    dils = [dil for _, dil in DILATED_GROUPS]
    assert dils[0] == 1
    xt = x.reshape(B * S, D)
    x1 = _ffn(xt, p["f1_pre"], p["f1_wi"], p["f1_wo"], p["f1_post"])
    hs = _norm_regroup(x1, S, p["mix_pre"], dils)
    h = hs[0].reshape(B * S, D)
    z0 = _in_proj(h, p["w_nat"])
    zs = [z0] + [_in_proj(hd.reshape(B * S, D), w) for hd, w in zip(hs[1:], p["w_groups"])]
    o_groups, lse_groups = [], []
    for g, (dil, z) in enumerate(zip(dils, zs)):
        o, lse = _dilated_group(z.reshape(B, dil, S // dil, z.shape[-1]), bias, g)
        o_groups.append(o)
        lse_groups.append(lse)
    z3 = z0.reshape(B, S, -1)
    f = _fourier(z3, wc, ws)
    m = _memory(z3, mem, p["mem_norm"], p["w_mem_kv"])
    a = _combine(S, o_groups, lse_groups)
    mg = _gated_merge(h, a, f, m, p["w_gate"], p["b_gate"], p["w_pa"], p["w_pf"], p["w_pm"])
    x2 = _out_proj(mg, x1, p["w_out"], p["mix_post"])
    x3 = _ffn(x2, p["f2_pre"], p["f2_wi"], p["f2_wo"], p["f2_post"])
    return x3.reshape(B, S, D)


def kernel(x_prompt, x_sample, mem_prompt, mem_sample, rel_bias, ffn1_norm_pre, ffn1_w_in, ffn1_w_out, ffn1_norm_post, mix_norm_pre, mem_norm, w_in, w_mem_kv, w_gate, b_gate, w_proj_attn, w_proj_fnet, w_proj_mem, w_out, mix_norm_post, ffn2_norm_pre, ffn2_w_in, ffn2_w_out, ffn2_norm_post):
    depth = w_in.shape[0]
    bias = _bias_tables(rel_bias)
    wc = _dft_table(FNET_GROUP_DIM, False, FNET_GROUP_DIM)
    ws = {s: _dft_table(s, True, 256) for s in {x_prompt.shape[1], x_sample.shape[1]}}
    y_prompt, y_sample = x_prompt, x_sample
    for l in range(depth):
        p = _layer_params(l, ffn1_norm_pre, ffn1_w_in, ffn1_w_out, ffn1_norm_post, mix_norm_pre, mem_norm,
                          w_in, w_mem_kv, w_gate, b_gate, w_proj_attn, w_proj_fnet, w_proj_mem, w_out,
                          mix_norm_post, ffn2_norm_pre, ffn2_w_in, ffn2_w_out, ffn2_norm_post)
        y_prompt = _layer(y_prompt, mem_prompt, bias, wc, ws[y_prompt.shape[1]], p)
        y_sample = _layer(y_sample, mem_sample, bias, wc, ws[y_sample.shape[1]], p)
    return (y_prompt, y_sample)
```

```python
import functools
import math

import jax
import jax.numpy as jnp
from jax import lax
from jax.experimental import pallas as pl
from jax.experimental.pallas import tpu as pltpu

F32 = jnp.float32
BF16 = jnp.bfloat16

RMS_EPS = 1e-6
NEG_INF = -1e30

DILATED_GROUPS = ((128, 1), (512, 4), (2048, 16))
GROUP_HEADS = 4
HEAD_DIM = 128
GROUP_WIDTH = GROUP_HEADS * HEAD_DIM
ATTN_WIDTH = len(DILATED_GROUPS) * GROUP_WIDTH
FNET_GROUPS = 4
FNET_GROUP_DIM = 256
FNET_WIDTH = FNET_GROUPS * FNET_GROUP_DIM
MEM_HEADS = 4
MEM_HEAD_DIM = 256
MEM_WIDTH = MEM_HEADS * MEM_HEAD_DIM
N_BUCKETS = 32
MAX_EXACT = 8
REL_MAX_DIST = 1024

Q_BLK = 128
HALF_WIN = 64
KEY_WIN = 256
COL_BLK = 512
NAT_FNET_COL = 3 * GROUP_WIDTH
NAT_MEM_COL = NAT_FNET_COL + FNET_WIDTH

V7X_VMEM_LIMIT_BYTES = 58 * 1024 * 1024
LANES = 128


def _cparams(semantics, vmem_bytes=V7X_VMEM_LIMIT_BYTES):
    return pltpu.CompilerParams(dimension_semantics=semantics, vmem_limit_bytes=vmem_bytes)


def _rms_f32(x, g):
    return x * lax.rsqrt(jnp.mean(x * x, axis=-1, keepdims=True) + RMS_EPS) * g


ROW_CHUNK = 16


def _for_row_chunks(nrows, fn):
    for k in range(nrows // ROW_CHUNK):
        fn(slice(k * ROW_CHUNK, (k + 1) * ROW_CHUNK))


def _ffn_body(x_ref, gpre_ref, wia_ref, wib_ref, wo_ref, gpost_ref, out_ref, h_scr):
    tm = x_ref.shape[0]
    j = pl.program_id(1)
    nj = pl.num_programs(1)

    def swiglu_out():
        h = h_scr[...]
        a = jnp.dot(h, wia_ref[...], preferred_element_type=F32)
        b = jnp.dot(h, wib_ref[...], preferred_element_type=F32)
        g = (a * jax.nn.sigmoid(a) * b).astype(BF16)
        return jnp.dot(g, wo_ref[...], preferred_element_type=F32)

    @pl.when(j == 0)
    def _():
        gpre = gpre_ref[...]

        def pre(rows):
            h_scr[rows, :] = _rms_f32(x_ref[rows, :], gpre).astype(BF16)

        _for_row_chunks(tm, pre)
        out_ref[...] = swiglu_out()

    @pl.when(j > 0)
    def _():
        out_ref[...] += swiglu_out()

    @pl.when(j == nj - 1)
    def _():
        gpost = 0.5 * gpost_ref[...]

        def post(rows):
            out_ref[rows, :] = x_ref[rows, :] + _rms_f32(out_ref[rows, :], gpost)

        _for_row_chunks(tm, post)


def _ffn(x, g_pre, w_in, w_out, g_post, *, tm=1024, tf=512):
    T, D = x.shape
    F = w_out.shape[0]
    nj = F // tf
    row = lambda i, j: (i, 0)
    const = lambda i, j: (0, 0)
    return pl.pallas_call(
        _ffn_body,
        out_shape=jax.ShapeDtypeStruct((T, D), F32),
        grid=(T // tm, nj),
        in_specs=[
            pl.BlockSpec((tm, D), row),
            pl.BlockSpec((1, D), const),
            pl.BlockSpec((D, tf), lambda i, j: (0, j)),
            pl.BlockSpec((D, tf), lambda i, j: (0, j + nj)),
            pl.BlockSpec((tf, D), lambda i, j: (j, 0)),
            pl.BlockSpec((1, D), const),
        ],
        out_specs=pl.BlockSpec((tm, D), row),
        scratch_shapes=[pltpu.VMEM((tm, D), BF16)],
        compiler_params=_cparams(("parallel", "arbitrary")),
        name="ffn",
    )(x, g_pre, w_in, w_in, w_out, g_post)


def _norm_regroup_body(dils, x_ref, g_ref, *rest):
    hn_refs = rest[:len(dils)]
    scratch = rest[len(dils):]
    tm = x_ref.shape[0]
    g = g_ref[...]

    def norm(rows):
        hn = _rms_f32(x_ref[rows, :], g)
        for hn_ref, dil in zip(hn_refs, dils):
            if dil == 1:
                hn_ref[0, 0, rows, :] = hn.astype(BF16)
        if scratch:
            for c in range(scratch[0].shape[0]):
                scratch[0][c, rows, :] = hn[:, c * LANES:(c + 1) * LANES]

    _for_row_chunks(tm, norm)
    prev_scr, prev_dil, level = (scratch[0] if scratch else None), 1, 1
    for hn_ref, dil in zip(hn_refs, dils):
        if dil == 1:
            continue
        ratio, n = dil // prev_dil, tm // dil
        cur_scr = scratch[level] if level < len(scratch) else None
        for rp in range(prev_dil):
            for m in range(ratio):
                r = rp + prev_dil * m
                for c in range(prev_scr.shape[0]):
                    piece = prev_scr[c, pl.ds(rp * (tm // prev_dil) + m, n, stride=ratio), :]
                    hn_ref[0, r, :, c * LANES:(c + 1) * LANES] = piece.astype(BF16)
                    if cur_scr is not None:
                        cur_scr[c, r * n:(r + 1) * n, :] = piece
        prev_scr, prev_dil, level = cur_scr, dil, level + 1


def _norm_regroup(x, seq, g, dils, *, tm=512):
    T, D = x.shape
    tps = seq // tm
    strided = [dil for dil in dils if dil > 1]
    assert all(b % a == 0 for a, b in zip([1] + strided, strided))
    return pl.pallas_call(
        functools.partial(_norm_regroup_body, tuple(dils)),
        out_shape=[jax.ShapeDtypeStruct((T // seq, dil, seq // dil, D), BF16) for dil in dils],
        grid=(T // tm,),
        in_specs=[pl.BlockSpec((tm, D), lambda i: (i, 0)), pl.BlockSpec((1, D), lambda i: (0, 0))],
        out_specs=[pl.BlockSpec((1, dil, tm // dil, D), lambda i: (i // tps, 0, i % tps, 0))
                   for dil in dils],
        scratch_shapes=[pltpu.VMEM((D // LANES, tm, LANES), F32)] * len(strided),
        compiler_params=_cparams(("parallel",)),
        name="norm_regroup",
    )(x, g)


def _matmul_body(h_ref, w_ref, o_ref):
    o_ref[...] = jnp.dot(h_ref[...], w_ref[...], preferred_element_type=F32).astype(o_ref.dtype)


def _in_proj(h, w, *, tm=1024, tn_max=1792):
    T, D = h.shape
    N = w.shape[1]
    tn = min(N, tn_max)
    return pl.pallas_call(
        _matmul_body,
        out_shape=jax.ShapeDtypeStruct((T, N), BF16),
        grid=(N // tn, T // tm),
        in_specs=[pl.BlockSpec((tm, D), lambda n, i: (i, 0)),
                  pl.BlockSpec((D, tn), lambda n, i: (0, n))],
        out_specs=pl.BlockSpec((tm, tn), lambda n, i: (i, n)),
        compiler_params=_cparams(("parallel", "parallel")),
        name="in_proj",
    )(h, w)


def _t5_bucket(rel):
    half = N_BUCKETS // 2
    n = jnp.abs(rel)
    nf = jnp.maximum(n, 1).astype(F32)
    large = MAX_EXACT + (jnp.log(nf / MAX_EXACT) / math.log(REL_MAX_DIST / MAX_EXACT)
                         * (half - MAX_EXACT)).astype(jnp.int32)
    large = jnp.minimum(large, half - 1)
    return jnp.where(rel > 0, half, 0) + jnp.where(n < MAX_EXACT, n, large)


WIN_OFFSETS = (-HALF_WIN, 0, Q_BLK - KEY_WIN)


def _bias_body(tbl_ref, out_ref):
    g = pl.program_id(0)
    w = pl.program_id(1)
    dil = jnp.where(g == 0, DILATED_GROUPS[0][1],
                    jnp.where(g == 1, DILATED_GROUPS[1][1], DILATED_GROUPS[2][1]))
    off = jnp.where(w == 0, WIN_OFFSETS[0], jnp.where(w == 1, WIN_OFFSETS[1], WIN_OFFSETS[2]))
    s = lax.broadcasted_iota(jnp.int32, (Q_BLK, KEY_WIN), 0)
    t = lax.broadcasted_iota(jnp.int32, (Q_BLK, KEY_WIN), 1)
    delta = t - s + off
    bucket = _t5_bucket(delta * dil)
    valid = jnp.abs(delta) <= HALF_WIN
    for h in range(GROUP_HEADS):
        col = g * GROUP_HEADS + h

        def pick(i, acc):
            return jnp.where(bucket == i, tbl_ref[i, col], acc)

        bias = lax.fori_loop(0, N_BUCKETS, pick, jnp.zeros((Q_BLK, KEY_WIN), F32))
        out_ref[0, 0, h] = jnp.where(valid, bias, NEG_INF)


def _bias_tables(rel_bias):
    ng = len(DILATED_GROUPS)
    return pl.pallas_call(
        _bias_body,
        out_shape=jax.ShapeDtypeStruct((ng, len(WIN_OFFSETS), GROUP_HEADS, Q_BLK, KEY_WIN), F32),
        grid=(ng, len(WIN_OFFSETS)),
        in_specs=[pl.BlockSpec(memory_space=pltpu.SMEM)],
        out_specs=pl.BlockSpec((1, 1, GROUP_HEADS, Q_BLK, KEY_WIN), lambda g, w: (g, w, 0, 0, 0)),
        compiler_params=_cparams(("arbitrary", "arbitrary")),
        name="rel_bias",
    )(rel_bias)


def _attn_body(L, TL, q_ref, k_ref, v_ref, bias_ref, o_ref, lse_ref):
    scale = HEAD_DIM ** -0.5
    nblk = L // Q_BLK
    per_res = TL // Q_BLK
    n0 = pl.program_id(2) * per_res
    kv_rows = k_ref.shape[1]
    ones = jnp.ones((KEY_WIN, HEAD_DIM), BF16)
    key_half = lax.broadcasted_iota(jnp.int32, (Q_BLK, KEY_WIN), 1) // Q_BLK

    def block(i, carry):
        rb = i // per_res
        n = n0 + i % per_res
        qs = pl.multiple_of(i * Q_BLK, Q_BLK)
        if L >= KEY_WIN:
            ks = rb * L + jnp.clip(n * Q_BLK - HALF_WIN, 0, L - KEY_WIN)
            kind = jnp.where(n == 0, 1, jnp.where(n == nblk - 1, 2, 0))
        else:
            assert L == Q_BLK and kv_rows >= KEY_WIN
            ks = jnp.minimum(rb * L, kv_rows - KEY_WIN)
            own_half = (rb * L - ks) // Q_BLK
            kind = 1 + own_half
        ks = pl.multiple_of(ks, HALF_WIN)
        for h in range(GROUP_HEADS):
            cols = slice(h * HEAD_DIM, (h + 1) * HEAD_DIM)
            qh = q_ref[0, pl.ds(qs, Q_BLK), cols]
            kh = k_ref[0, pl.ds(ks, KEY_WIN), cols]
            vh = v_ref[0, pl.ds(ks, KEY_WIN), cols]
            s = lax.dot_general(qh, kh, (((1,), (1,)), ((), ())), preferred_element_type=F32)
            s = s * scale + bias_ref[0, kind, h]
            if L < KEY_WIN:
                s = jnp.where(key_half == own_half, s, NEG_INF)
            m = jnp.max(s, axis=-1, keepdims=True)
            p = jnp.exp(s - m).astype(BF16)
            od = jnp.dot(p, jnp.concatenate([vh, ones], axis=1), preferred_element_type=F32)
            den = od[:, HEAD_DIM:]
            o_ref[0, pl.ds(qs, Q_BLK), cols] = (od[:, :HEAD_DIM] / den).astype(o_ref.dtype)
            lse_ref[0, pl.ds(qs, Q_BLK), cols] = m + jnp.log(den)
        return carry

    lax.fori_loop(0, q_ref.shape[1] // Q_BLK, block, 0, unroll=4)


ATTN_ROWS_PER_STEP = 1024


def _dilated_group(zg, bias, g):
    B, dil, L, C = zg.shape
    TL = min(ATTN_ROWS_PER_STEP, L)
    RB = min(dil, ATTN_ROWS_PER_STEP // TL)
    z3 = zg.reshape(B, dil * L, C)
    kv_blk = (1, RB * L, GROUP_WIDTH)
    q_blk = (1, RB * TL, GROUP_WIDTH)
    q_idx = lambda b, r, c: (b, r * (L // TL) + c, 0)
    out_sds = lambda dtype: jax.ShapeDtypeStruct((B, dil * L, GROUP_WIDTH), dtype)
    o, lse = pl.pallas_call(
        functools.partial(_attn_body, L, TL),
        out_shape=[out_sds(BF16), out_sds(F32)],
        grid=(B, dil // RB, L // TL),
        in_specs=[
            pl.BlockSpec(q_blk, q_idx),
            pl.BlockSpec(kv_blk, lambda b, r, c: (b, r, 1)),
            pl.BlockSpec(kv_blk, lambda b, r, c: (b, r, 2)),
            pl.BlockSpec((1, len(WIN_OFFSETS), GROUP_HEADS, Q_BLK, KEY_WIN),
                         lambda b, r, c: (g, 0, 0, 0, 0)),
        ],
        out_specs=[pl.BlockSpec(q_blk, q_idx), pl.BlockSpec(q_blk, q_idx)],
        compiler_params=_cparams(("parallel", "parallel", "arbitrary")),
        name=f"dilated_attn_g{g}",
    )(z3, z3, z3, bias)
    return o.reshape(B, dil, L, GROUP_WIDTH), lse.reshape(B, dil, L, GROUP_WIDTH)


def _dft_angle(r, c, n):
    return ((r * c) & (n - 1)).astype(F32) * (2.0 * math.pi / n)


def _dft_base_body(n, cos_ref, sin_ref):
    rows, ncols = cos_ref.shape
    ang = _dft_angle(lax.broadcasted_iota(jnp.int32, (rows, ncols), 0),
                     lax.broadcasted_iota(jnp.int32, (rows, ncols), 1), n)
    cos_ref[...] = jnp.cos(ang)
    sin_ref[...] = jnp.sin(ang)


def _dft_table_body(n, neg_sin, cb_ref, sb_ref, o_ref):
    rows, ncols = cb_ref.shape
    ang0 = _dft_angle(pl.program_id(0) * rows, lax.broadcasted_iota(jnp.int32, (1, ncols), 1), n)
    c0, s0 = jnp.cos(ang0), jnp.sin(ang0)
    cb, sb = cb_ref[...], sb_ref[...]
    sn = s0 * cb + c0 * sb
    o_ref[:, :ncols] = (c0 * cb - s0 * sb).astype(o_ref.dtype)
    o_ref[:, ncols:] = (-sn if neg_sin else sn).astype(o_ref.dtype)


def _dft_table(n, ncols, neg_sin, rows):
    assert n & (n - 1) == 0
    base_sds = jax.ShapeDtypeStruct((rows, ncols), F32)
    cb, sb = pl.pallas_call(
        functools.partial(_dft_base_body, n),
        out_shape=[base_sds, base_sds],
        compiler_params=_cparams(()),
        name=f"dft_base_{n}",
    )()
    return pl.pallas_call(
        functools.partial(_dft_table_body, n, neg_sin),
        out_shape=jax.ShapeDtypeStruct((n, 2 * ncols), BF16),
        grid=(n // rows,),
        in_specs=[pl.BlockSpec((rows, ncols), lambda i: (0, 0))] * 2,
        out_specs=pl.BlockSpec((rows, 2 * ncols), lambda i: (i, 0)),
        compiler_params=_cparams(("parallel",)),
        name=f"dft_table_{n}",
    )(cb, sb)


def _dft_tables(seq_lens):
    wc = _dft_table(FNET_GROUP_DIM, FNET_GROUP_DIM, False, FNET_GROUP_DIM)
    return wc, {s: _dft_table(s, s // 2, True, 256) for s in seq_lens}


FOLD_ROWS = 128


def _fourier_body(S, ua_ref, ub_ref, wc_ref, ws_ref, o_ref, y_scr, ymid_scr):
    C = FNET_GROUP_DIM
    H = S // 2
    R = FOLD_ROWS
    tr = o_ref.shape[1]
    i = pl.program_id(1)

    @pl.when(i == 0)
    def _():
        flip = (lax.broadcasted_iota(jnp.int32, (R, 2 * R), 1)
                == R - lax.broadcasted_iota(jnp.int32, (R, 2 * R), 0)).astype(BF16)
        cosc, sinc = wc_ref[:, :C], wc_ref[:, C:]
        for u_ref, base in ((ua_ref, 0), (ub_ref, COL_BLK)):
            for b in range(H // R):
                if b == 0:
                    rev = jnp.dot(flip[:, :R], u_ref[0, S - R:S, :], preferred_element_type=F32)
                else:
                    rev = jnp.dot(flip, u_ref[0, S - (b + 1) * R:S - (b - 1) * R, :],
                                  preferred_element_type=F32)
                u = u_ref[0, b * R:(b + 1) * R, :].astype(F32)
                usum, udif = (u + rev).astype(BF16), (u - rev).astype(BF16)
                for k in range(COL_BLK // C):
                    c0 = base + k * C
                    y_scr[b * R:(b + 1) * R, c0:c0 + C] = jnp.dot(
                        usum[:, k * C:(k + 1) * C], cosc, preferred_element_type=F32).astype(BF16)
                    y_scr[H + b * R:H + (b + 1) * R, c0:c0 + C] = jnp.dot(
                        udif[:, k * C:(k + 1) * C], sinc, preferred_element_type=F32).astype(BF16)
            for k in range(COL_BLK // C):
                c0 = base + k * C
                ymid_scr[:, c0:c0 + C] = jnp.dot(u_ref[0, H:H + ROW_CHUNK, k * C:(k + 1) * C], cosc,
                                                 preferred_element_type=F32)

    pos = i * tr + lax.broadcasted_iota(jnp.int32, (tr, 1), 0)
    sign = (1 - 2 * (pos & 1)).astype(F32)
    f = jnp.dot(ws_ref[...], y_scr[...], preferred_element_type=F32) + sign * ymid_scr[0:1, :]
    o_ref[0] = (f * (1.0 / math.sqrt(S * C))).astype(o_ref.dtype)


def _fourier(z3, wc, ws, *, tr=512):
    B, S, C = z3.shape
    cb = NAT_FNET_COL // COL_BLK
    assert ws.shape == (S, S) and (S // 2) % FOLD_ROWS == 0
    return pl.pallas_call(
        functools.partial(_fourier_body, S),
        out_shape=jax.ShapeDtypeStruct((B, S, FNET_WIDTH), BF16),
        grid=(B, S // tr),
        in_specs=[
            pl.BlockSpec((1, S, COL_BLK), lambda b, i: (b, 0, cb)),
            pl.BlockSpec((1, S, COL_BLK), lambda b, i: (b, 0, cb + 1)),
            pl.BlockSpec((FNET_GROUP_DIM, 2 * FNET_GROUP_DIM), lambda b, i: (0, 0)),
            pl.BlockSpec((tr, S), lambda b, i: (i, 0)),
        ],
        out_specs=pl.BlockSpec((1, tr, FNET_WIDTH), lambda b, i: (b, i, 0)),
        scratch_shapes=[pltpu.VMEM((S, FNET_WIDTH), BF16), pltpu.VMEM((ROW_CHUNK, FNET_WIDTH), F32)],
        compiler_params=_cparams(("parallel", "arbitrary")),
        name="fourier",
    )(z3, z3, wc, ws).reshape(B * S, FNET_WIDTH)


def _memory_body(qa_ref, qb_ref, mem_ref, gm_ref, wkv_ref, o_ref, k_scr, v_scr):
    E = MEM_HEAD_DIM

    @pl.when(pl.program_id(1) == 0)
    def _():
        hm = _rms_f32(mem_ref[0], gm_ref[...]).astype(BF16)
        kv = jnp.dot(hm, wkv_ref[...], preferred_element_type=F32)
        k_scr[...] = (kv[:, :MEM_WIDTH] * (E ** -0.5)).astype(BF16)
        v_scr[...] = kv[:, MEM_WIDTH:].astype(BF16)

    for q_ref, base in ((qa_ref, 0), (qb_ref, COL_BLK)):
        for k in range(COL_BLK // E):
            c0 = base + k * E
            s = lax.dot_general(q_ref[0, :, k * E:(k + 1) * E], k_scr[:, c0:c0 + E],
                                (((1,), (1,)), ((), ())), preferred_element_type=F32)
            m = jnp.max(s, axis=-1, keepdims=True)
            p = jnp.exp(s - m)
            den = jnp.sum(p, axis=-1, keepdims=True)
            o = jnp.dot(p.astype(BF16), v_scr[:, c0:c0 + E], preferred_element_type=F32) / den
            o_ref[0, :, c0:c0 + E] = o.astype(o_ref.dtype)


def _memory(z3, mem, g_mem, w_kv, *, tq=1024):
    B, S, C = z3.shape
    M, D = mem.shape[1:]
    cb = NAT_MEM_COL // COL_BLK
    assert MEM_HEAD_DIM ** -0.5 == 2.0 ** round(math.log2(MEM_HEAD_DIM ** -0.5))
    return pl.pallas_call(
        _memory_body,
        out_shape=jax.ShapeDtypeStruct((B, S, MEM_WIDTH), BF16),
        grid=(B, S // tq),
        in_specs=[
            pl.BlockSpec((1, tq, COL_BLK), lambda b, i: (b, i, cb)),
            pl.BlockSpec((1, tq, COL_BLK), lambda b, i: (b, i, cb + 1)),
            pl.BlockSpec((1, M, D), lambda b, i: (b, 0, 0)),
            pl.BlockSpec((1, D), lambda b, i: (0, 0)),
            pl.BlockSpec((D, 2 * MEM_WIDTH), lambda b, i: (0, 0)),
        ],
        out_specs=pl.BlockSpec((1, tq, MEM_WIDTH), lambda b, i: (b, i, 0)),
        scratch_shapes=[pltpu.VMEM((M, MEM_WIDTH), BF16), pltpu.VMEM((M, MEM_WIDTH), BF16)],
        compiler_params=_cparams(("parallel", "arbitrary")),
        name="memory_attn",
    )(z3, z3, mem, g_mem, w_kv).reshape(B * S, MEM_WIDTH)


def _combine_body(o0_ref, o1_ref, o2_ref, l0_ref, l1_ref, l2_ref, a_ref, nat_scr):
    tm = a_ref.shape[0]
    nc = GROUP_WIDTH // LANES

    def to_natural(ref, slot):
        dil = ref.shape[1]
        for r in range(dil):
            for c in range(nc):
                nat_scr[slot * nc + c, pl.ds(r, tm // dil, stride=dil), :] = (
                    ref[0, r, :, c * LANES:(c + 1) * LANES].astype(F32))

    def natural_rows(ref, slot, rows):
        if ref.shape[1] == 1:
            return ref[0, 0, rows, :].astype(F32)
        return jnp.concatenate([nat_scr[slot * nc + c, rows, :] for c in range(nc)], axis=1)

    slots = {}
    for ref in (l1_ref, l2_ref, o1_ref, o2_ref):
        slots[id(ref)] = len(slots)
        to_natural(ref, slots[id(ref)])

    def combine(rows):
        l0 = natural_rows(l0_ref, None, rows)
        l1 = natural_rows(l1_ref, slots[id(l1_ref)], rows)
        l2 = natural_rows(l2_ref, slots[id(l2_ref)], rows)
        mx = jnp.maximum(jnp.maximum(l0, l1), l2)
        e0, e1, e2 = jnp.exp(l0 - mx), jnp.exp(l1 - mx), jnp.exp(l2 - mx)
        num = (e0 * natural_rows(o0_ref, None, rows)
               + e1 * natural_rows(o1_ref, slots[id(o1_ref)], rows)
               + e2 * natural_rows(o2_ref, slots[id(o2_ref)], rows))
        a_ref[rows, :] = (num / (e0 + e1 + e2)).astype(BF16)

    _for_row_chunks(tm, combine)


def _combine(seq, o_groups, lse_groups, *, tm=512):
    B = o_groups[0].shape[0]
    tps = seq // tm
    group_specs = [pl.BlockSpec((1, a.shape[1], tm // a.shape[1], GROUP_WIDTH),
                                lambda i: (i // tps, 0, i % tps, 0))
                   for a in (*o_groups, *lse_groups)]
    return pl.pallas_call(
        _combine_body,
        out_shape=jax.ShapeDtypeStruct((B * seq, GROUP_WIDTH), BF16),
        grid=(B * tps,),
        in_specs=group_specs,
        out_specs=pl.BlockSpec((tm, GROUP_WIDTH), lambda i: (i, 0)),
        scratch_shapes=[pltpu.VMEM((2 * (len(DILATED_GROUPS) - 1) * GROUP_WIDTH // LANES, tm, LANES), F32)],
        compiler_params=_cparams(("parallel",)),
        name="group_combine",
    )(*o_groups, *lse_groups)


def _gated_body(h_ref, a_ref, f_ref, m_ref, wg0_ref, wg1_ref, wg2_ref, bg0_ref, bg1_ref, bg2_ref,
                wpa_ref, wpf_ref, wpm_ref, out_ref):
    h = h_ref[...]

    def gate(w_ref, b_ref):
        return jax.nn.sigmoid(jnp.dot(h, w_ref[...], preferred_element_type=F32) + b_ref[...])

    mg = gate(wg0_ref, bg0_ref) * jnp.dot(a_ref[...], wpa_ref[...], preferred_element_type=F32)
    mg += gate(wg1_ref, bg1_ref) * jnp.dot(f_ref[...], wpf_ref[...], preferred_element_type=F32)
    mg += gate(wg2_ref, bg2_ref) * jnp.dot(m_ref[...], wpm_ref[...], preferred_element_type=F32)
    out_ref[...] = mg.astype(out_ref.dtype)


def _gated_merge(h, a, f, m, w_gate, b_gate, w_pa, w_pf, w_pm, *, tm=1024, tn=256):
    T, D = h.shape
    nj = D // tn
    row = lambda i, j: (i, 0)
    colj = lambda i, j: (0, j)
    gate_specs = [pl.BlockSpec((D, tn), lambda i, j, k=k: (0, j + k * nj)) for k in range(3)]
    bias_specs = [pl.BlockSpec((1, tn), lambda i, j, k=k: (0, j + k * nj)) for k in range(3)]
    return pl.pallas_call(
        _gated_body,
        out_shape=jax.ShapeDtypeStruct((T, D), BF16),
        grid=(T // tm, nj),
        in_specs=[pl.BlockSpec((tm, D), row), pl.BlockSpec((tm, GROUP_WIDTH), row),
                  pl.BlockSpec((tm, FNET_WIDTH), row), pl.BlockSpec((tm, MEM_WIDTH), row)]
                 + gate_specs + bias_specs
                 + [pl.BlockSpec((GROUP_WIDTH, tn), colj), pl.BlockSpec((FNET_WIDTH, tn), colj),
                    pl.BlockSpec((MEM_WIDTH, tn), colj)],
        out_specs=pl.BlockSpec((tm, tn), lambda i, j: (i, j)),
        compiler_params=_cparams(("parallel", "parallel")),
        name="gated_merge",
    )(h, a, f, m, w_gate, w_gate, w_gate, b_gate, b_gate, b_gate, w_pa, w_pf, w_pm)


def _out_proj_body(mg_ref, x_ref, wo_ref, gpost_ref, out_ref, y_scr):
    y_scr[...] = jnp.dot(mg_ref[...], wo_ref[...], preferred_element_type=F32)
    gpost = gpost_ref[...]

    def post(rows):
        out_ref[rows, :] = x_ref[rows, :] + _rms_f32(y_scr[rows, :], gpost)

    _for_row_chunks(x_ref.shape[0], post)


def _out_proj(mg, x, w_out, g_post, *, tm=512):
    T, D = x.shape
    row = lambda i: (i, 0)
    const = lambda i: (0, 0)
    return pl.pallas_call(
        _out_proj_body,
        out_shape=jax.ShapeDtypeStruct((T, D), F32),
        grid=(T // tm,),
        in_specs=[pl.BlockSpec((tm, D), row), pl.BlockSpec((tm, D), row),
                  pl.BlockSpec((D, D), const), pl.BlockSpec((1, D), const)],
        out_specs=pl.BlockSpec((tm, D), row),
        scratch_shapes=[pltpu.VMEM((tm, D), F32)],
        compiler_params=_cparams(("parallel",)),
        name="out_proj",
    )(mg, x, w_out, g_post)


def _layer_params(l, ffn1_norm_pre, ffn1_w_in, ffn1_w_out, ffn1_norm_post, mix_norm_pre, mem_norm, w_in,
                  w_mem_kv, w_gate, b_gate, w_proj_attn, w_proj_fnet, w_proj_mem, w_out, mix_norm_post,
                  ffn2_norm_pre, ffn2_w_in, ffn2_w_out, ffn2_norm_post):
    vec = lambda a: a[l][None, :]
    mat = lambda a: a[l].astype(BF16)
    wi = mat(w_in)
    qkv = lambda g: [wi[:, s * ATTN_WIDTH + g * GROUP_WIDTH: s * ATTN_WIDTH + (g + 1) * GROUP_WIDTH]
                     for s in range(3)]
    w_nat = jnp.concatenate(qkv(0) + [wi[:, 3 * ATTN_WIDTH:]], axis=1)
    w_groups = [jnp.concatenate(qkv(g), axis=1) for g in range(1, len(DILATED_GROUPS))]
    return dict(f1_pre=vec(ffn1_norm_pre), f1_wi=mat(ffn1_w_in), f1_wo=mat(ffn1_w_out),
                f1_post=vec(ffn1_norm_post), mix_pre=vec(mix_norm_pre), mem_norm=vec(mem_norm),
                w_nat=w_nat, w_groups=w_groups, w_mem_kv=mat(w_mem_kv), w_gate=mat(w_gate),
                b_gate=vec(b_gate), w_pa=mat(w_proj_attn), w_pf=mat(w_proj_fnet),
                w_pm=mat(w_proj_mem), w_out=mat(w_out), mix_post=vec(mix_norm_post),
                f2_pre=vec(ffn2_norm_pre), f2_wi=mat(ffn2_w_in), f2_wo=mat(ffn2_w_out),
                f2_post=vec(ffn2_norm_post))


def _layer(x, mem, bias, wc, ws, p):
    B, S, D = x.shape
    dils = [dil for _, dil in DILATED_GROUPS]
    assert dils[0] == 1
    xt = x.reshape(B * S, D)
    x1 = _ffn(xt, p["f1_pre"], p["f1_wi"], p["f1_wo"], p["f1_post"])
    hs = _norm_regroup(x1, S, p["mix_pre"], dils)
    h = hs[0].reshape(B * S, D)
    z0 = _in_proj(h, p["w_nat"])
    zs = [z0] + [_in_proj(hd.reshape(B * S, D), w) for hd, w in zip(hs[1:], p["w_groups"])]
    o_groups, lse_groups = [], []
    for g, (dil, z) in enumerate(zip(dils, zs)):
        o, lse = _dilated_group(z.reshape(B, dil, S // dil, z.shape[-1]), bias, g)
        o_groups.append(o)
        lse_groups.append(lse)
    z3 = z0.reshape(B, S, -1)
    f = _fourier(z3, wc, ws)
    m = _memory(z3, mem, p["mem_norm"], p["w_mem_kv"])
    a = _combine(S, o_groups, lse_groups)
    mg = _gated_merge(h, a, f, m, p["w_gate"], p["b_gate"], p["w_pa"], p["w_pf"], p["w_pm"])
    x2 = _out_proj(mg, x1, p["w_out"], p["mix_post"])
    x3 = _ffn(x2, p["f2_pre"], p["f2_wi"], p["f2_wo"], p["f2_post"])
    return x3.reshape(B, S, D)


def kernel(x_prompt, x_sample, mem_prompt, mem_sample, rel_bias, ffn1_norm_pre, ffn1_w_in, ffn1_w_out, ffn1_norm_post, mix_norm_pre, mem_norm, w_in, w_mem_kv, w_gate, b_gate, w_proj_attn, w_proj_fnet, w_proj_mem, w_out, mix_norm_post, ffn2_norm_pre, ffn2_w_in, ffn2_w_out, ffn2_norm_post):
    depth = w_in.shape[0]
    bias = _bias_tables(rel_bias)
    wc, ws = _dft_tables({x_prompt.shape[1], x_sample.shape[1]})
    y_prompt, y_sample = x_prompt, x_sample
    for l in range(depth):
        p = _layer_params(l, ffn1_norm_pre, ffn1_w_in, ffn1_w_out, ffn1_norm_post, mix_norm_pre, mem_norm,
                          w_in, w_mem_kv, w_gate, b_gate, w_proj_attn, w_proj_fnet, w_proj_mem, w_out,
                          mix_norm_post, ffn2_norm_pre, ffn2_w_in, ffn2_w_out, ffn2_norm_post)
        y_prompt = _layer(y_prompt, mem_prompt, bias, wc, ws[y_prompt.shape[1]], p)
        y_sample = _layer(y_sample, mem_sample, bias, wc, ws[y_sample.shape[1]], p)
    return (y_prompt, y_sample)
```

```python
import functools
import math

import jax
import jax.numpy as jnp
from jax import lax
from jax.experimental import pallas as pl
from jax.experimental.pallas import tpu as pltpu

F32 = jnp.float32
BF16 = jnp.bfloat16

RMS_EPS = 1e-6
NEG_INF = -1e30

DILATED_GROUPS = ((128, 1), (512, 4), (2048, 16))
GROUP_HEADS = 4
HEAD_DIM = 128
GROUP_WIDTH = GROUP_HEADS * HEAD_DIM
ATTN_WIDTH = len(DILATED_GROUPS) * GROUP_WIDTH
FNET_GROUPS = 4
FNET_GROUP_DIM = 256
FNET_WIDTH = FNET_GROUPS * FNET_GROUP_DIM
MEM_HEADS = 4
MEM_HEAD_DIM = 256
MEM_WIDTH = MEM_HEADS * MEM_HEAD_DIM
N_BUCKETS = 32
MAX_EXACT = 8
REL_MAX_DIST = 1024

Q_BLK = 128
HALF_WIN = 64
KEY_WIN = 256
COL_BLK = 512
NAT_FNET_COL = 3 * GROUP_WIDTH
NAT_MEM_COL = NAT_FNET_COL + FNET_WIDTH

V7X_VMEM_LIMIT_BYTES = 58 * 1024 * 1024
LANES = 128


def _cparams(semantics, vmem_bytes=V7X_VMEM_LIMIT_BYTES):
    return pltpu.CompilerParams(dimension_semantics=semantics, vmem_limit_bytes=vmem_bytes)


def _rms_f32(x, g):
    return x * lax.rsqrt(jnp.mean(x * x, axis=-1, keepdims=True) + RMS_EPS) * g


ROW_CHUNK = 16


def _for_row_chunks(nrows, fn):
    for k in range(nrows // ROW_CHUNK):
        fn(slice(k * ROW_CHUNK, (k + 1) * ROW_CHUNK))


def _ffn_body(x_ref, gpre_ref, wia_ref, wib_ref, wo_ref, gpost_ref, out_ref, h_scr):
    tm = x_ref.shape[0]
    j = pl.program_id(1)
    nj = pl.num_programs(1)

    def swiglu_out():
        h = h_scr[...]
        a = jnp.dot(h, wia_ref[...], preferred_element_type=F32)
        b = jnp.dot(h, wib_ref[...], preferred_element_type=F32)
        g = (a * jax.nn.sigmoid(a) * b).astype(BF16)
        return jnp.dot(g, wo_ref[...], preferred_element_type=F32)

    @pl.when(j == 0)
    def _():
        gpre = gpre_ref[...]

        def pre(rows):
            h_scr[rows, :] = _rms_f32(x_ref[rows, :], gpre).astype(BF16)

        _for_row_chunks(tm, pre)
        out_ref[...] = swiglu_out()

    @pl.when(j > 0)
    def _():
        out_ref[...] += swiglu_out()

    @pl.when(j == nj - 1)
    def _():
        gpost = 0.5 * gpost_ref[...]

        def post(rows):
            out_ref[rows, :] = x_ref[rows, :] + _rms_f32(out_ref[rows, :], gpost)

        _for_row_chunks(tm, post)


def _ffn(x, g_pre, w_in, w_out, g_post, *, tm=1024, tf=512):
    T, D = x.shape
    F = w_out.shape[0]
    nj = F // tf
    row = lambda i, j: (i, 0)
    const = lambda i, j: (0, 0)
    return pl.pallas_call(
        _ffn_body,
        out_shape=jax.ShapeDtypeStruct((T, D), F32),
        grid=(T // tm, nj),
        in_specs=[
            pl.BlockSpec((tm, D), row),
            pl.BlockSpec((1, D), const),
            pl.BlockSpec((D, tf), lambda i, j: (0, j)),
            pl.BlockSpec((D, tf), lambda i, j: (0, j + nj)),
            pl.BlockSpec((tf, D), lambda i, j: (j, 0)),
            pl.BlockSpec((1, D), const),
        ],
        out_specs=pl.BlockSpec((tm, D), row),
        scratch_shapes=[pltpu.VMEM((tm, D), BF16)],
        compiler_params=_cparams(("parallel", "arbitrary")),
        name="ffn",
    )(x, g_pre, w_in, w_in, w_out, g_post)


def _norm_regroup_body(dils, x_ref, g_ref, *rest):
    hn_refs = rest[:len(dils)]
    scratch = rest[len(dils):]
    tm = x_ref.shape[0]
    g = g_ref[...]

    def norm(rows):
        hn = _rms_f32(x_ref[rows, :], g)
        for hn_ref, dil in zip(hn_refs, dils):
            if dil == 1:
                hn_ref[0, 0, rows, :] = hn.astype(BF16)
        if scratch:
            for c in range(scratch[0].shape[0]):
                scratch[0][c, rows, :] = hn[:, c * LANES:(c + 1) * LANES]

    _for_row_chunks(tm, norm)
    prev_scr, prev_dil, level = (scratch[0] if scratch else None), 1, 1
    for hn_ref, dil in zip(hn_refs, dils):
        if dil == 1:
            continue
        ratio, n = dil // prev_dil, tm // dil
        cur_scr = scratch[level] if level < len(scratch) else None
        for rp in range(prev_dil):
            for m in range(ratio):
                r = rp + prev_dil * m
                for c in range(prev_scr.shape[0]):
                    piece = prev_scr[c, pl.ds(rp * (tm // prev_dil) + m, n, stride=ratio), :]
                    hn_ref[0, r, :, c * LANES:(c + 1) * LANES] = piece.astype(BF16)
                    if cur_scr is not None:
                        cur_scr[c, r * n:(r + 1) * n, :] = piece
        prev_scr, prev_dil, level = cur_scr, dil, level + 1


def _norm_regroup(x, seq, g, dils, *, tm=512):
    T, D = x.shape
    tps = seq // tm
    strided = [dil for dil in dils if dil > 1]
    assert all(b % a == 0 for a, b in zip([1] + strided, strided))
    return pl.pallas_call(
        functools.partial(_norm_regroup_body, tuple(dils)),
        out_shape=[jax.ShapeDtypeStruct((T // seq, dil, seq // dil, D), BF16) for dil in dils],
        grid=(T // tm,),
        in_specs=[pl.BlockSpec((tm, D), lambda i: (i, 0)), pl.BlockSpec((1, D), lambda i: (0, 0))],
        out_specs=[pl.BlockSpec((1, dil, tm // dil, D), lambda i: (i // tps, 0, i % tps, 0))
                   for dil in dils],
        scratch_shapes=[pltpu.VMEM((D // LANES, tm, LANES), F32)] * len(strided),
        compiler_params=_cparams(("parallel",)),
        name="norm_regroup",
    )(x, g)


def _matmul_body(h_ref, w_ref, o_ref):
    o_ref[...] = jnp.dot(h_ref[...], w_ref[...], preferred_element_type=F32).astype(o_ref.dtype)


def _in_proj(h, w, *, tm=1024, tn_max=1792):
    T, D = h.shape
    N = w.shape[1]
    tn = min(N, tn_max)
    return pl.pallas_call(
        _matmul_body,
        out_shape=jax.ShapeDtypeStruct((T, N), BF16),
        grid=(N // tn, T // tm),
        in_specs=[pl.BlockSpec((tm, D), lambda n, i: (i, 0)),
                  pl.BlockSpec((D, tn), lambda n, i: (0, n))],
        out_specs=pl.BlockSpec((tm, tn), lambda n, i: (i, n)),
        compiler_params=_cparams(("parallel", "parallel")),
        name="in_proj",
    )(h, w)


def _t5_bucket(rel):
    half = N_BUCKETS // 2
    n = jnp.abs(rel)
    nf = jnp.maximum(n, 1).astype(F32)
    large = MAX_EXACT + (jnp.log(nf / MAX_EXACT) / math.log(REL_MAX_DIST / MAX_EXACT)
                         * (half - MAX_EXACT)).astype(jnp.int32)
    large = jnp.minimum(large, half - 1)
    return jnp.where(rel > 0, half, 0) + jnp.where(n < MAX_EXACT, n, large)


WIN_OFFSETS = (-HALF_WIN, 0, Q_BLK - KEY_WIN)


def _bias_body(tbl_ref, out_ref):
    g = pl.program_id(0)
    w = pl.program_id(1)
    dil = jnp.where(g == 0, DILATED_GROUPS[0][1],
                    jnp.where(g == 1, DILATED_GROUPS[1][1], DILATED_GROUPS[2][1]))
    off = jnp.where(w == 0, WIN_OFFSETS[0], jnp.where(w == 1, WIN_OFFSETS[1], WIN_OFFSETS[2]))
    s = lax.broadcasted_iota(jnp.int32, (Q_BLK, KEY_WIN), 0)
    t = lax.broadcasted_iota(jnp.int32, (Q_BLK, KEY_WIN), 1)
    delta = t - s + off
    bucket = _t5_bucket(delta * dil)
    valid = jnp.abs(delta) <= HALF_WIN
    for h in range(GROUP_HEADS):
        col = g * GROUP_HEADS + h

        def pick(i, acc):
            return jnp.where(bucket == i, tbl_ref[i, col], acc)

        bias = lax.fori_loop(0, N_BUCKETS, pick, jnp.zeros((Q_BLK, KEY_WIN), F32))
        out_ref[0, 0, h] = jnp.where(valid, bias, NEG_INF)


def _bias_tables(rel_bias):
    ng = len(DILATED_GROUPS)
    return pl.pallas_call(
        _bias_body,
        out_shape=jax.ShapeDtypeStruct((ng, len(WIN_OFFSETS), GROUP_HEADS, Q_BLK, KEY_WIN), F32),
        grid=(ng, len(WIN_OFFSETS)),
        in_specs=[pl.BlockSpec(memory_space=pltpu.SMEM)],
        out_specs=pl.BlockSpec((1, 1, GROUP_HEADS, Q_BLK, KEY_WIN), lambda g, w: (g, w, 0, 0, 0)),
        compiler_params=_cparams(("arbitrary", "arbitrary")),
        name="rel_bias",
    )(rel_bias)


def _attn_body(L, TL, q_ref, k_ref, v_ref, bias_ref, o_ref, lse_ref):
    scale = HEAD_DIM ** -0.5
    nblk = L // Q_BLK
    per_res = TL // Q_BLK
    n0 = pl.program_id(2) * per_res
    kv_rows = k_ref.shape[1]
    ones = jnp.ones((KEY_WIN, HEAD_DIM), BF16)
    key_half = lax.broadcasted_iota(jnp.int32, (Q_BLK, KEY_WIN), 1) // Q_BLK

    def block(i, carry):
        rb = i // per_res
        n = n0 + i % per_res
        qs = pl.multiple_of(i * Q_BLK, Q_BLK)
        if L >= KEY_WIN:
            ks = rb * L + jnp.clip(n * Q_BLK - HALF_WIN, 0, L - KEY_WIN)
            kind = jnp.where(n == 0, 1, jnp.where(n == nblk - 1, 2, 0))
        else:
            assert L == Q_BLK and kv_rows >= KEY_WIN
            ks = jnp.minimum(rb * L, kv_rows - KEY_WIN)
            own_half = (rb * L - ks) // Q_BLK
            kind = 1 + own_half
        ks = pl.multiple_of(ks, HALF_WIN)
        for h in range(GROUP_HEADS):
            cols = slice(h * HEAD_DIM, (h + 1) * HEAD_DIM)
            qh = q_ref[0, pl.ds(qs, Q_BLK), cols]
            kh = k_ref[0, pl.ds(ks, KEY_WIN), cols]
            vh = v_ref[0, pl.ds(ks, KEY_WIN), cols]
            s = lax.dot_general(qh, kh, (((1,), (1,)), ((), ())), preferred_element_type=F32)
            s = s * scale + bias_ref[0, kind, h]
            if L < KEY_WIN:
                s = jnp.where(key_half == own_half, s, NEG_INF)
            m = jnp.max(s, axis=-1, keepdims=True)
            p = jnp.exp(s - m).astype(BF16)
            od = jnp.dot(p, jnp.concatenate([vh, ones], axis=1), preferred_element_type=F32)
            den = od[:, HEAD_DIM:]
            o_ref[0, pl.ds(qs, Q_BLK), cols] = (od[:, :HEAD_DIM] / den).astype(o_ref.dtype)
            lse_ref[0, pl.ds(qs, Q_BLK), cols] = m + jnp.log(den)
        return carry

    lax.fori_loop(0, q_ref.shape[1] // Q_BLK, block, 0, unroll=4)


ATTN_ROWS_PER_STEP = 1024


def _dilated_group(zg, bias, g):
    B, dil, L, C = zg.shape
    TL = min(ATTN_ROWS_PER_STEP, L)
    RB = min(dil, ATTN_ROWS_PER_STEP // TL)
    z3 = zg.reshape(B, dil * L, C)
    kv_blk = (1, RB * L, GROUP_WIDTH)
    q_blk = (1, RB * TL, GROUP_WIDTH)
    q_idx = lambda b, r, c: (b, r * (L // TL) + c, 0)
    out_sds = lambda dtype: jax.ShapeDtypeStruct((B, dil * L, GROUP_WIDTH), dtype)
    o, lse = pl.pallas_call(
        functools.partial(_attn_body, L, TL),
        out_shape=[out_sds(BF16), out_sds(F32)],
        grid=(B, dil // RB, L // TL),
        in_specs=[
            pl.BlockSpec(q_blk, q_idx),
            pl.BlockSpec(kv_blk, lambda b, r, c: (b, r, 1)),
            pl.BlockSpec(kv_blk, lambda b, r, c: (b, r, 2)),
            pl.BlockSpec((1, len(WIN_OFFSETS), GROUP_HEADS, Q_BLK, KEY_WIN),
                         lambda b, r, c: (g, 0, 0, 0, 0)),
        ],
        out_specs=[pl.BlockSpec(q_blk, q_idx), pl.BlockSpec(q_blk, q_idx)],
        compiler_params=_cparams(("parallel", "parallel", "arbitrary")),
        name=f"dilated_attn_g{g}",
    )(z3, z3, z3, bias)
    return o.reshape(B, dil, L, GROUP_WIDTH), lse.reshape(B, dil, L, GROUP_WIDTH)


def _dft_angle(r, c, n):
    return ((r * c) & (n - 1)).astype(F32) * (2.0 * math.pi / n)


def _dft_base_body(n, cos_ref, sin_ref):
    rows, ncols = cos_ref.shape
    ang = _dft_angle(lax.broadcasted_iota(jnp.int32, (rows, ncols), 0),
                     lax.broadcasted_iota(jnp.int32, (rows, ncols), 1), n)
    cos_ref[...] = jnp.cos(ang)
    sin_ref[...] = jnp.sin(ang)


def _dft_table_body(n, neg_sin, cb_ref, sb_ref, o_ref):
    rows, ncols = cb_ref.shape
    ang0 = _dft_angle(pl.program_id(0) * rows, lax.broadcasted_iota(jnp.int32, (1, ncols), 1), n)
    c0, s0 = jnp.cos(ang0), jnp.sin(ang0)
    cb, sb = cb_ref[...], sb_ref[...]
    sn = s0 * cb + c0 * sb
    o_ref[:, :ncols] = (c0 * cb - s0 * sb).astype(o_ref.dtype)
    o_ref[:, ncols:] = (-sn if neg_sin else sn).astype(o_ref.dtype)


def _dft_table(n, ncols, neg_sin, rows):
    assert n & (n - 1) == 0
    base_sds = jax.ShapeDtypeStruct((rows, ncols), F32)
    cb, sb = pl.pallas_call(
        functools.partial(_dft_base_body, n),
        out_shape=[base_sds, base_sds],
        compiler_params=_cparams(()),
        name=f"dft_base_{n}",
    )()
    return pl.pallas_call(
        functools.partial(_dft_table_body, n, neg_sin),
        out_shape=jax.ShapeDtypeStruct((n, 2 * ncols), BF16),
        grid=(n // rows,),
        in_specs=[pl.BlockSpec((rows, ncols), lambda i: (0, 0))] * 2,
        out_specs=pl.BlockSpec((rows, 2 * ncols), lambda i: (i, 0)),
        compiler_params=_cparams(("parallel",)),
        name=f"dft_table_{n}",
    )(cb, sb)


def _dft_tables(seq_lens):
    wc = _dft_table(FNET_GROUP_DIM, FNET_GROUP_DIM, False, FNET_GROUP_DIM)
    return wc, {s: _dft_table(s, s // 2, True, 256) for s in seq_lens}


FOLD_ROWS = 128


def _fourier_body(S, ua_ref, ub_ref, wc_ref, ws_ref, lo_ref, hi_ref, y_scr, ymid_scr, g_scr):
    C = FNET_GROUP_DIM
    H = S // 2
    R = FOLD_ROWS
    tr = lo_ref.shape[1]
    t = pl.program_id(1)
    i = pl.num_programs(1) - 1 - t
    scale = 1.0 / math.sqrt(S * C)
    flip = (lax.broadcasted_iota(jnp.int32, (R, 2 * R), 1)
            == R - lax.broadcasted_iota(jnp.int32, (R, 2 * R), 0)).astype(BF16)

    @pl.when(t == 0)
    def _():
        cosc, sinc = wc_ref[:, :C], wc_ref[:, C:]
        for u_ref, base in ((ua_ref, 0), (ub_ref, COL_BLK)):
            for b in range(H // R):
                if b == 0:
                    rev = jnp.dot(flip[:, :R], u_ref[0, S - R:S, :], preferred_element_type=F32)
                else:
                    rev = jnp.dot(flip, u_ref[0, S - (b + 1) * R:S - (b - 1) * R, :],
                                  preferred_element_type=F32)
                u = u_ref[0, b * R:(b + 1) * R, :].astype(F32)
                usum, udif = (u + rev).astype(BF16), (u - rev).astype(BF16)
                for k in range(COL_BLK // C):
                    c0 = base + k * C
                    y_scr[b * R:(b + 1) * R, c0:c0 + C] = jnp.dot(
                        usum[:, k * C:(k + 1) * C], cosc, preferred_element_type=F32).astype(BF16)
                    y_scr[H + b * R:H + (b + 1) * R, c0:c0 + C] = jnp.dot(
                        udif[:, k * C:(k + 1) * C], sinc, preferred_element_type=F32).astype(BF16)
            for k in range(COL_BLK // C):
                c0 = base + k * C
                ymid_scr[:, c0:c0 + C] = jnp.dot(u_ref[0, H:H + ROW_CHUNK, k * C:(k + 1) * C], cosc,
                                                 preferred_element_type=F32)
        col = lax.broadcasted_iota(jnp.int32, (ROW_CHUNK, H), 1)
        alt = (1 - 2 * (col & 1)).astype(BF16)
        p_mid = jnp.dot(alt, y_scr[:H, :], preferred_element_type=F32) + ymid_scr[...]
        g_scr[tr:tr + R, :] = jnp.zeros((R, g_scr.shape[1]), BF16)
        g_scr[tr:tr + ROW_CHUNK, :] = (p_mid * scale).astype(BF16)

    pos = i * tr + lax.broadcasted_iota(jnp.int32, (tr, 1), 0)
    sign = (1 - 2 * (pos & 1)).astype(F32)
    p = jnp.dot(ws_ref[:, :H], y_scr[:H, :], preferred_element_type=F32) + sign * ymid_scr[0:1, :]
    qn = jnp.dot(ws_ref[:, H:], y_scr[H:, :], preferred_element_type=F32)
    lo_ref[0] = ((p + qn) * scale).astype(lo_ref.dtype)
    g_scr[:tr, :] = ((p - qn) * scale).astype(BF16)
    for u in range(tr // R):
        hi_ref[0, u * R:(u + 1) * R, :] = jnp.dot(
            flip, g_scr[tr - (u + 1) * R:tr - (u - 1) * R, :], preferred_element_type=F32
        ).astype(hi_ref.dtype)
    g_scr[tr:tr + R, :] = g_scr[:R, :]


def _fourier(z3, wc, ws, *, tr=512):
    B, S, C = z3.shape
    H = S // 2
    nb = H // tr
    cb = NAT_FNET_COL // COL_BLK
    assert ws.shape == (S, S) and H % FOLD_ROWS == 0 and tr % FOLD_ROWS == 0
    half_sds = jax.ShapeDtypeStruct((B, H, FNET_WIDTH), BF16)
    return pl.pallas_call(
        functools.partial(_fourier_body, S),
        out_shape=[half_sds, half_sds],
        grid=(B, nb),
        in_specs=[
            pl.BlockSpec((1, S, COL_BLK), lambda b, t: (b, 0, cb)),
            pl.BlockSpec((1, S, COL_BLK), lambda b, t: (b, 0, cb + 1)),
            pl.BlockSpec((FNET_GROUP_DIM, 2 * FNET_GROUP_DIM), lambda b, t: (0, 0)),
            pl.BlockSpec((tr, S), lambda b, t: (nb - 1 - t, 0)),
        ],
        out_specs=[pl.BlockSpec((1, tr, FNET_WIDTH), lambda b, t: (b, nb - 1 - t, 0)),
                   pl.BlockSpec((1, tr, FNET_WIDTH), lambda b, t: (b, t, 0))],
        scratch_shapes=[pltpu.VMEM((S, FNET_WIDTH), BF16), pltpu.VMEM((ROW_CHUNK, FNET_WIDTH), F32),
                        pltpu.VMEM((tr + FOLD_ROWS, FNET_WIDTH), BF16)],
        compiler_params=_cparams(("parallel", "arbitrary")),
        name="fourier",
    )(z3, z3, wc, ws)


def _memory_body(qa_ref, qb_ref, mem_ref, gm_ref, wkv_ref, o_ref, k_scr, v_scr):
    E = MEM_HEAD_DIM

    @pl.when(pl.program_id(1) == 0)
    def _():
        hm = _rms_f32(mem_ref[0], gm_ref[...]).astype(BF16)
        kv = jnp.dot(hm, wkv_ref[...], preferred_element_type=F32)
        k_scr[...] = (kv[:, :MEM_WIDTH] * (E ** -0.5)).astype(BF16)
        v_scr[...] = kv[:, MEM_WIDTH:].astype(BF16)

    for q_ref, base in ((qa_ref, 0), (qb_ref, COL_BLK)):
        for k in range(COL_BLK // E):
            c0 = base + k * E
            s = lax.dot_general(q_ref[0, :, k * E:(k + 1) * E], k_scr[:, c0:c0 + E],
                                (((1,), (1,)), ((), ())), preferred_element_type=F32)
            m = jnp.max(s, axis=-1, keepdims=True)
            p = jnp.exp(s - m)
            den = jnp.sum(p, axis=-1, keepdims=True)
            o = jnp.dot(p.astype(BF16), v_scr[:, c0:c0 + E], preferred_element_type=F32) / den
            o_ref[0, :, c0:c0 + E] = o.astype(o_ref.dtype)


def _memory(z3, mem, g_mem, w_kv, *, tq=1024):
    B, S, C = z3.shape
    M, D = mem.shape[1:]
    cb = NAT_MEM_COL // COL_BLK
    assert MEM_HEAD_DIM ** -0.5 == 2.0 ** round(math.log2(MEM_HEAD_DIM ** -0.5))
    return pl.pallas_call(
        _memory_body,
        out_shape=jax.ShapeDtypeStruct((B, S, MEM_WIDTH), BF16),
        grid=(B, S // tq),
        in_specs=[
            pl.BlockSpec((1, tq, COL_BLK), lambda b, i: (b, i, cb)),
            pl.BlockSpec((1, tq, COL_BLK), lambda b, i: (b, i, cb + 1)),
            pl.BlockSpec((1, M, D), lambda b, i: (b, 0, 0)),
            pl.BlockSpec((1, D), lambda b, i: (0, 0)),
            pl.BlockSpec((D, 2 * MEM_WIDTH), lambda b, i: (0, 0)),
        ],
        out_specs=pl.BlockSpec((1, tq, MEM_WIDTH), lambda b, i: (b, i, 0)),
        scratch_shapes=[pltpu.VMEM((M, MEM_WIDTH), BF16), pltpu.VMEM((M, MEM_WIDTH), BF16)],
        compiler_params=_cparams(("parallel", "arbitrary")),
        name="memory_attn",
    )(z3, z3, mem, g_mem, w_kv).reshape(B * S, MEM_WIDTH)


def _combine_body(o0_ref, o1_ref, o2_ref, l0_ref, l1_ref, l2_ref, a_ref, nat_scr):
    tm = a_ref.shape[0]
    nc = GROUP_WIDTH // LANES

    def to_natural(ref, slot):
        dil = ref.shape[1]
        for r in range(dil):
            for c in range(nc):
                nat_scr[slot * nc + c, pl.ds(r, tm // dil, stride=dil), :] = (
                    ref[0, r, :, c * LANES:(c + 1) * LANES].astype(F32))

    def natural_rows(ref, slot, rows):
        if ref.shape[1] == 1:
            return ref[0, 0, rows, :].astype(F32)
        return jnp.concatenate([nat_scr[slot * nc + c, rows, :] for c in range(nc)], axis=1)

    slots = {}
    for ref in (l1_ref, l2_ref, o1_ref, o2_ref):
        slots[id(ref)] = len(slots)
        to_natural(ref, slots[id(ref)])

    def combine(rows):
        l0 = natural_rows(l0_ref, None, rows)
        l1 = natural_rows(l1_ref, slots[id(l1_ref)], rows)
        l2 = natural_rows(l2_ref, slots[id(l2_ref)], rows)
        mx = jnp.maximum(jnp.maximum(l0, l1), l2)
        e0, e1, e2 = jnp.exp(l0 - mx), jnp.exp(l1 - mx), jnp.exp(l2 - mx)
        num = (e0 * natural_rows(o0_ref, None, rows)
               + e1 * natural_rows(o1_ref, slots[id(o1_ref)], rows)
               + e2 * natural_rows(o2_ref, slots[id(o2_ref)], rows))
        a_ref[rows, :] = (num / (e0 + e1 + e2)).astype(BF16)

    _for_row_chunks(tm, combine)


def _combine(seq, o_groups, lse_groups, *, tm=512):
    B = o_groups[0].shape[0]
    tps = seq // tm
    group_specs = [pl.BlockSpec((1, a.shape[1], tm // a.shape[1], GROUP_WIDTH),
                                lambda i: (i // tps, 0, i % tps, 0))
                   for a in (*o_groups, *lse_groups)]
    return pl.pallas_call(
        _combine_body,
        out_shape=jax.ShapeDtypeStruct((B * seq, GROUP_WIDTH), BF16),
        grid=(B * tps,),
        in_specs=group_specs,
        out_specs=pl.BlockSpec((tm, GROUP_WIDTH), lambda i: (i, 0)),
        scratch_shapes=[pltpu.VMEM((2 * (len(DILATED_GROUPS) - 1) * GROUP_WIDTH // LANES, tm, LANES), F32)],
        compiler_params=_cparams(("parallel",)),
        name="group_combine",
    )(*o_groups, *lse_groups)


def _gated_body(tiles_per_half, h_ref, a_ref, flo_ref, fhi_ref, m_ref, wg0_ref, wg1_ref, wg2_ref,
                bg0_ref, bg1_ref, bg2_ref, wpa_ref, wpf_ref, wpm_ref, out_ref):
    h = h_ref[...]
    in_first_half = (pl.program_id(0) // tiles_per_half) % 2 == 0
    f = jnp.where(in_first_half, flo_ref[...], fhi_ref[...])

    def mm(x, w_ref):
        return jnp.dot(x, w_ref[...].astype(BF16), preferred_element_type=F32)

    def gate(w_ref, b_ref):
        return jax.nn.sigmoid(mm(h, w_ref) + b_ref[...])

    mg = gate(wg0_ref, bg0_ref) * mm(a_ref[...], wpa_ref)
    mg += gate(wg1_ref, bg1_ref) * mm(f, wpf_ref)
    mg += gate(wg2_ref, bg2_ref) * mm(m_ref[...], wpm_ref)
    out_ref[...] = mg.astype(out_ref.dtype)


def _gated_merge(h, a, f_halves, m, w_gate, b_gate, w_pa, w_pf, w_pm, *, tm=1024, tn=256):
    T, D = h.shape
    nj = D // tn
    B, H, _ = f_halves[0].shape
    tph = H // tm
    assert tph >= 1 and T == 2 * B * H
    f_lo, f_hi = (x.reshape(B * H, FNET_WIDTH) for x in f_halves)
    row = lambda i, j: (i, 0)
    colj = lambda i, j: (0, j)
    half_row = lambda i, j: ((i // (2 * tph)) * tph + i % tph, 0)
    gate_specs = [pl.BlockSpec((D, tn), lambda i, j, k=k: (0, j + k * nj)) for k in range(3)]
    bias_specs = [pl.BlockSpec((1, tn), lambda i, j, k=k: (0, j + k * nj)) for k in range(3)]
    return pl.pallas_call(
        functools.partial(_gated_body, tph),
        out_shape=jax.ShapeDtypeStruct((T, D), BF16),
        grid=(T // tm, nj),
        in_specs=[pl.BlockSpec((tm, D), row), pl.BlockSpec((tm, GROUP_WIDTH), row),
                  pl.BlockSpec((tm, FNET_WIDTH), half_row), pl.BlockSpec((tm, FNET_WIDTH), half_row),
                  pl.BlockSpec((tm, MEM_WIDTH), row)]
                 + gate_specs + bias_specs
                 + [pl.BlockSpec((GROUP_WIDTH, tn), colj), pl.BlockSpec((FNET_WIDTH, tn), colj),
                    pl.BlockSpec((MEM_WIDTH, tn), colj)],
        out_specs=pl.BlockSpec((tm, tn), lambda i, j: (i, j)),
        compiler_params=_cparams(("parallel", "parallel")),
        name="gated_merge",
    )(h, a, f_lo, f_hi, m, w_gate, w_gate, w_gate, b_gate, b_gate, b_gate, w_pa, w_pf, w_pm)


def _out_proj_body(mg_ref, x_ref, wo_ref, gpost_ref, out_ref, y_scr):
    y_scr[...] = jnp.dot(mg_ref[...], wo_ref[...], preferred_element_type=F32)
    gpost = gpost_ref[...]

    def post(rows):
        out_ref[rows, :] = x_ref[rows, :] + _rms_f32(y_scr[rows, :], gpost)

    _for_row_chunks(x_ref.shape[0], post)


def _out_proj(mg, x, w_out, g_post, *, tm=512):
    T, D = x.shape
    row = lambda i: (i, 0)
    const = lambda i: (0, 0)
    return pl.pallas_call(
        _out_proj_body,
        out_shape=jax.ShapeDtypeStruct((T, D), F32),
        grid=(T // tm,),
        in_specs=[pl.BlockSpec((tm, D), row), pl.BlockSpec((tm, D), row),
                  pl.BlockSpec((D, D), const), pl.BlockSpec((1, D), const)],
        out_specs=pl.BlockSpec((tm, D), row),
        scratch_shapes=[pltpu.VMEM((tm, D), F32)],
        compiler_params=_cparams(("parallel",)),
        name="out_proj",
    )(mg, x, w_out, g_post)


def _layer_params(l, ffn1_norm_pre, ffn1_w_in, ffn1_w_out, ffn1_norm_post, mix_norm_pre, mem_norm, w_in,
                  w_mem_kv, w_gate, b_gate, w_proj_attn, w_proj_fnet, w_proj_mem, w_out, mix_norm_post,
                  ffn2_norm_pre, ffn2_w_in, ffn2_w_out, ffn2_norm_post):
    vec = lambda a: a[l][None, :]
    mat = lambda a: a[l].astype(BF16)
    wi = mat(w_in)
    qkv = lambda g: [wi[:, s * ATTN_WIDTH + g * GROUP_WIDTH: s * ATTN_WIDTH + (g + 1) * GROUP_WIDTH]
                     for s in range(3)]
    w_nat = jnp.concatenate(qkv(0) + [wi[:, 3 * ATTN_WIDTH:]], axis=1)
    w_groups = [jnp.concatenate(qkv(g), axis=1) for g in range(1, len(DILATED_GROUPS))]
    return dict(f1_pre=vec(ffn1_norm_pre), f1_wi=mat(ffn1_w_in), f1_wo=mat(ffn1_w_out),
                f1_post=vec(ffn1_norm_post), mix_pre=vec(mix_norm_pre), mem_norm=vec(mem_norm),
                w_nat=w_nat, w_groups=w_groups, w_mem_kv=mat(w_mem_kv), w_gate=w_gate[l],
                b_gate=vec(b_gate), w_pa=w_proj_attn[l], w_pf=w_proj_fnet[l],
                w_pm=w_proj_mem[l], w_out=mat(w_out), mix_post=vec(mix_norm_post),
                f2_pre=vec(ffn2_norm_pre), f2_wi=mat(ffn2_w_in), f2_wo=mat(ffn2_w_out),
                f2_post=vec(ffn2_norm_post))


def _layer(x, mem, bias, wc, ws, p):
    B, S, D = x.shape
    dils = [dil for _, dil in DILATED_GROUPS]
    assert dils[0] == 1
    xt = x.reshape(B * S, D)
    x1 = _ffn(xt, p["f1_pre"], p["f1_wi"], p["f1_wo"], p["f1_post"])
    hs = _norm_regroup(x1, S, p["mix_pre"], dils)
    h = hs[0].reshape(B * S, D)
    z0 = _in_proj(h, p["w_nat"])
    zs = [z0] + [_in_proj(hd.reshape(B * S, D), w) for hd, w in zip(hs[1:], p["w_groups"])]
    o_groups, lse_groups = [], []
    for g, (dil, z) in enumerate(zip(dils, zs)):
        o, lse = _dilated_group(z.reshape(B, dil, S // dil, z.shape[-1]), bias, g)
        o_groups.append(o)
        lse_groups.append(lse)
    z3 = z0.reshape(B, S, -1)
    f = _fourier(z3, wc, ws, tr=min(512, S // 4))
    m = _memory(z3, mem, p["mem_norm"], p["w_mem_kv"])
    a = _combine(S, o_groups, lse_groups)
    mg = _gated_merge(h, a, f, m, p["w_gate"], p["b_gate"], p["w_pa"], p["w_pf"], p["w_pm"])
    x2 = _out_proj(mg, x1, p["w_out"], p["mix_post"])
    x3 = _ffn(x2, p["f2_pre"], p["f2_wi"], p["f2_wo"], p["f2_post"])
    return x3.reshape(B, S, D)


def kernel(x_prompt, x_sample, mem_prompt, mem_sample, rel_bias, ffn1_norm_pre, ffn1_w_in, ffn1_w_out, ffn1_norm_post, mix_norm_pre, mem_norm, w_in, w_mem_kv, w_gate, b_gate, w_proj_attn, w_proj_fnet, w_proj_mem, w_out, mix_norm_post, ffn2_norm_pre, ffn2_w_in, ffn2_w_out, ffn2_norm_post):
    depth = w_in.shape[0]
    bias = _bias_tables(rel_bias)
    wc, ws = _dft_tables({x_prompt.shape[1], x_sample.shape[1]})
    y_prompt, y_sample = x_prompt, x_sample
    for l in range(depth):
        p = _layer_params(l, ffn1_norm_pre, ffn1_w_in, ffn1_w_out, ffn1_norm_post, mix_norm_pre, mem_norm,
                          w_in, w_mem_kv, w_gate, b_gate, w_proj_attn, w_proj_fnet, w_proj_mem, w_out,
                          mix_norm_post, ffn2_norm_pre, ffn2_w_in, ffn2_w_out, ffn2_norm_post)
        y_prompt = _layer(y_prompt, mem_prompt, bias, wc, ws[y_prompt.shape[1]], p)
        y_sample = _layer(y_sample, mem_sample, bias, wc, ws[y_sample.shape[1]], p)
    return (y_prompt, y_sample)
```

```python
import functools
import math

import jax
import jax.numpy as jnp
from jax import lax
from jax.experimental import pallas as pl
from jax.experimental.pallas import tpu as pltpu

F32 = jnp.float32
BF16 = jnp.bfloat16

RMS_EPS = 1e-6
NEG_INF = -1e30

DILATED_GROUPS = ((128, 1), (512, 4), (2048, 16))
GROUP_HEADS = 4
HEAD_DIM = 128
GROUP_WIDTH = GROUP_HEADS * HEAD_DIM
ATTN_WIDTH = len(DILATED_GROUPS) * GROUP_WIDTH
FNET_GROUPS = 4
FNET_GROUP_DIM = 256
FNET_WIDTH = FNET_GROUPS * FNET_GROUP_DIM
MEM_HEADS = 4
MEM_HEAD_DIM = 256
MEM_WIDTH = MEM_HEADS * MEM_HEAD_DIM
N_BUCKETS = 32
MAX_EXACT = 8
REL_MAX_DIST = 1024

Q_BLK = 128
HALF_WIN = 64
KEY_WIN = 256
COL_BLK = 512
NAT_FNET_COL = 3 * GROUP_WIDTH
NAT_MEM_COL = NAT_FNET_COL + FNET_WIDTH

V7X_VMEM_LIMIT_BYTES = 58 * 1024 * 1024
LANES = 128


def _cparams(semantics, vmem_bytes=V7X_VMEM_LIMIT_BYTES):
    return pltpu.CompilerParams(dimension_semantics=semantics, vmem_limit_bytes=vmem_bytes)


def _rms_f32(x, g):
    return x * lax.rsqrt(jnp.mean(x * x, axis=-1, keepdims=True) + RMS_EPS) * g


ROW_CHUNK = 16


def _for_row_chunks(nrows, fn):
    for k in range(nrows // ROW_CHUNK):
        fn(slice(k * ROW_CHUNK, (k + 1) * ROW_CHUNK))


def _ffn_body(x_ref, gpre_ref, wia_ref, wib_ref, wo_ref, gpost_ref, out_ref, h_scr):
    tm = x_ref.shape[0]
    j = pl.program_id(1)
    nj = pl.num_programs(1)

    def swiglu_out():
        h = h_scr[...]
        a = jnp.dot(h, wia_ref[...], preferred_element_type=F32)
        b = jnp.dot(h, wib_ref[...], preferred_element_type=F32)
        g = (a * jax.nn.sigmoid(a) * b).astype(BF16)
        return jnp.dot(g, wo_ref[...], preferred_element_type=F32)

    @pl.when(j == 0)
    def _():
        gpre = gpre_ref[...]

        def pre(rows):
            h_scr[rows, :] = _rms_f32(x_ref[rows, :], gpre).astype(BF16)

        _for_row_chunks(tm, pre)
        out_ref[...] = swiglu_out()

    @pl.when(j > 0)
    def _():
        out_ref[...] += swiglu_out()

    @pl.when(j == nj - 1)
    def _():
        gpost = 0.5 * gpost_ref[...]

        def post(rows):
            out_ref[rows, :] = x_ref[rows, :] + _rms_f32(out_ref[rows, :], gpost)

        _for_row_chunks(tm, post)


def _ffn(x, g_pre, w_in, w_out, g_post, *, tm=1024, tf=512):
    T, D = x.shape
    F = w_out.shape[0]
    nj = F // tf
    row = lambda i, j: (i, 0)
    const = lambda i, j: (0, 0)
    return pl.pallas_call(
        _ffn_body,
        out_shape=jax.ShapeDtypeStruct((T, D), F32),
        grid=(T // tm, nj),
        in_specs=[
            pl.BlockSpec((tm, D), row),
            pl.BlockSpec((1, D), const),
            pl.BlockSpec((D, tf), lambda i, j: (0, j)),
            pl.BlockSpec((D, tf), lambda i, j: (0, j + nj)),
            pl.BlockSpec((tf, D), lambda i, j: (j, 0)),
            pl.BlockSpec((1, D), const),
        ],
        out_specs=pl.BlockSpec((tm, D), row),
        scratch_shapes=[pltpu.VMEM((tm, D), BF16)],
        compiler_params=_cparams(("parallel", "arbitrary")),
        name="ffn",
    )(x, g_pre, w_in, w_in, w_out, g_post)


def _norm_regroup_body(dils, x_ref, g_ref, *rest):
    hn_refs = rest[:len(dils)]
    scratch = rest[len(dils):]
    tm = x_ref.shape[0]
    g = g_ref[...]

    def norm(rows):
        hn = _rms_f32(x_ref[rows, :], g)
        for hn_ref, dil in zip(hn_refs, dils):
            if dil == 1:
                hn_ref[0, 0, rows, :] = hn.astype(BF16)
        if scratch:
            for c in range(scratch[0].shape[0]):
                scratch[0][c, rows, :] = hn[:, c * LANES:(c + 1) * LANES]

    _for_row_chunks(tm, norm)
    prev_scr, prev_dil, level = (scratch[0] if scratch else None), 1, 1
    for hn_ref, dil in zip(hn_refs, dils):
        if dil == 1:
            continue
        ratio, n = dil // prev_dil, tm // dil
        cur_scr = scratch[level] if level < len(scratch) else None
        for rp in range(prev_dil):
            for m in range(ratio):
                r = rp + prev_dil * m
                for c in range(prev_scr.shape[0]):
                    piece = prev_scr[c, pl.ds(rp * (tm // prev_dil) + m, n, stride=ratio), :]
                    hn_ref[0, r, :, c * LANES:(c + 1) * LANES] = piece.astype(BF16)
                    if cur_scr is not None:
                        cur_scr[c, r * n:(r + 1) * n, :] = piece
        prev_scr, prev_dil, level = cur_scr, dil, level + 1


def _norm_regroup(x, seq, g, dils, *, tm=512):
    T, D = x.shape
    tps = seq // tm
    strided = [dil for dil in dils if dil > 1]
    assert all(b % a == 0 for a, b in zip([1] + strided, strided))
    return pl.pallas_call(
        functools.partial(_norm_regroup_body, tuple(dils)),
        out_shape=[jax.ShapeDtypeStruct((T // seq, dil, seq // dil, D), BF16) for dil in dils],
        grid=(T // tm,),
        in_specs=[pl.BlockSpec((tm, D), lambda i: (i, 0)), pl.BlockSpec((1, D), lambda i: (0, 0))],
        out_specs=[pl.BlockSpec((1, dil, tm // dil, D), lambda i: (i // tps, 0, i % tps, 0))
                   for dil in dils],
        scratch_shapes=[pltpu.VMEM((D // LANES, tm, LANES), F32)] * len(strided),
        compiler_params=_cparams(("parallel",)),
        name="norm_regroup",
    )(x, g)


def _matmul_body(h_ref, w_ref, o_ref):
    o_ref[...] = jnp.dot(h_ref[...], w_ref[...], preferred_element_type=F32).astype(o_ref.dtype)


def _in_proj(h, w, *, tm=1024, tn_max=1792):
    T, D = h.shape
    N = w.shape[1]
    tn = min(N, tn_max)
    return pl.pallas_call(
        _matmul_body,
        out_shape=jax.ShapeDtypeStruct((T, N), BF16),
        grid=(N // tn, T // tm),
        in_specs=[pl.BlockSpec((tm, D), lambda n, i: (i, 0)),
                  pl.BlockSpec((D, tn), lambda n, i: (0, n))],
        out_specs=pl.BlockSpec((tm, tn), lambda n, i: (i, n)),
        compiler_params=_cparams(("parallel", "parallel")),
        name="in_proj",
    )(h, w)


def _t5_bucket(rel):
    half = N_BUCKETS // 2
    n = jnp.abs(rel)
    nf = jnp.maximum(n, 1).astype(F32)
    large = MAX_EXACT + (jnp.log(nf / MAX_EXACT) / math.log(REL_MAX_DIST / MAX_EXACT)
                         * (half - MAX_EXACT)).astype(jnp.int32)
    large = jnp.minimum(large, half - 1)
    return jnp.where(rel > 0, half, 0) + jnp.where(n < MAX_EXACT, n, large)


WIN_OFFSETS = (-HALF_WIN, 0, Q_BLK - KEY_WIN)


def _bias_body(tbl_ref, out_ref):
    g = pl.program_id(0)
    w = pl.program_id(1)
    dil = jnp.where(g == 0, DILATED_GROUPS[0][1],
                    jnp.where(g == 1, DILATED_GROUPS[1][1], DILATED_GROUPS[2][1]))
    off = jnp.where(w == 0, WIN_OFFSETS[0], jnp.where(w == 1, WIN_OFFSETS[1], WIN_OFFSETS[2]))
    s = lax.broadcasted_iota(jnp.int32, (Q_BLK, KEY_WIN), 0)
    t = lax.broadcasted_iota(jnp.int32, (Q_BLK, KEY_WIN), 1)
    delta = t - s + off
    bucket = _t5_bucket(delta * dil)
    valid = jnp.abs(delta) <= HALF_WIN
    for h in range(GROUP_HEADS):
        col = g * GROUP_HEADS + h

        def pick(i, acc):
            return jnp.where(bucket == i, tbl_ref[i, col], acc)

        bias = lax.fori_loop(0, N_BUCKETS, pick, jnp.zeros((Q_BLK, KEY_WIN), F32))
        out_ref[0, 0, h] = jnp.where(valid, bias, NEG_INF)


def _bias_tables(rel_bias):
    ng = len(DILATED_GROUPS)
    return pl.pallas_call(
        _bias_body,
        out_shape=jax.ShapeDtypeStruct((ng, len(WIN_OFFSETS), GROUP_HEADS, Q_BLK, KEY_WIN), F32),
        grid=(ng, len(WIN_OFFSETS)),
        in_specs=[pl.BlockSpec(memory_space=pltpu.SMEM)],
        out_specs=pl.BlockSpec((1, 1, GROUP_HEADS, Q_BLK, KEY_WIN), lambda g, w: (g, w, 0, 0, 0)),
        compiler_params=_cparams(("arbitrary", "arbitrary")),
        name="rel_bias",
    )(rel_bias)


def _attn_body(L, TL, q_ref, k_ref, v_ref, bias_ref, o_ref, lse_ref):
    scale = HEAD_DIM ** -0.5
    nblk = L // Q_BLK
    per_res = TL // Q_BLK
    n0 = pl.program_id(2) * per_res
    kv_rows = k_ref.shape[1]
    ones = jnp.ones((KEY_WIN, HEAD_DIM), BF16)
    key_half = lax.broadcasted_iota(jnp.int32, (Q_BLK, KEY_WIN), 1) // Q_BLK

    def block(i, carry):
        rb = i // per_res
        n = n0 + i % per_res
        qs = pl.multiple_of(i * Q_BLK, Q_BLK)
        if L >= KEY_WIN:
            ks = rb * L + jnp.clip(n * Q_BLK - HALF_WIN, 0, L - KEY_WIN)
            kind = jnp.where(n == 0, 1, jnp.where(n == nblk - 1, 2, 0))
        else:
            assert L == Q_BLK and kv_rows >= KEY_WIN
            ks = jnp.minimum(rb * L, kv_rows - KEY_WIN)
            own_half = (rb * L - ks) // Q_BLK
            kind = 1 + own_half
        ks = pl.multiple_of(ks, HALF_WIN)
        for h in range(GROUP_HEADS):
            cols = slice(h * HEAD_DIM, (h + 1) * HEAD_DIM)
            qh = q_ref[0, pl.ds(qs, Q_BLK), cols]
            kh = k_ref[0, pl.ds(ks, KEY_WIN), cols]
            vh = v_ref[0, pl.ds(ks, KEY_WIN), cols]
            s = lax.dot_general(qh, kh, (((1,), (1,)), ((), ())), preferred_element_type=F32)
            s = s * scale + bias_ref[0, kind, h]
            if L < KEY_WIN:
                s = jnp.where(key_half == own_half, s, NEG_INF)
            m = jnp.max(s, axis=-1, keepdims=True)
            p = jnp.exp(s - m).astype(BF16)
            od = jnp.dot(p, jnp.concatenate([vh, ones], axis=1), preferred_element_type=F32)
            den = od[:, HEAD_DIM:]
            o_ref[0, pl.ds(qs, Q_BLK), cols] = (od[:, :HEAD_DIM] / den).astype(o_ref.dtype)
            lse_ref[0, pl.ds(qs, Q_BLK), cols] = m + jnp.log(den)
        return carry

    lax.fori_loop(0, q_ref.shape[1] // Q_BLK, block, 0, unroll=4)


ATTN_ROWS_PER_STEP = 1024


def _dilated_group(zg, bias, g):
    B, dil, L, C = zg.shape
    TL = min(ATTN_ROWS_PER_STEP, L)
    RB = min(dil, ATTN_ROWS_PER_STEP // TL)
    z3 = zg.reshape(B, dil * L, C)
    kv_blk = (1, RB * L, GROUP_WIDTH)
    q_blk = (1, RB * TL, GROUP_WIDTH)
    q_idx = lambda b, r, c: (b, r * (L // TL) + c, 0)
    out_sds = lambda dtype: jax.ShapeDtypeStruct((B, dil * L, GROUP_WIDTH), dtype)
    o, lse = pl.pallas_call(
        functools.partial(_attn_body, L, TL),
        out_shape=[out_sds(BF16), out_sds(F32)],
        grid=(B, dil // RB, L // TL),
        in_specs=[
            pl.BlockSpec(q_blk, q_idx),
            pl.BlockSpec(kv_blk, lambda b, r, c: (b, r, 1)),
            pl.BlockSpec(kv_blk, lambda b, r, c: (b, r, 2)),
            pl.BlockSpec((1, len(WIN_OFFSETS), GROUP_HEADS, Q_BLK, KEY_WIN),
                         lambda b, r, c: (g, 0, 0, 0, 0)),
        ],
        out_specs=[pl.BlockSpec(q_blk, q_idx), pl.BlockSpec(q_blk, q_idx)],
        compiler_params=_cparams(("parallel", "parallel", "arbitrary")),
        name=f"dilated_attn_g{g}",
    )(z3, z3, z3, bias)
    return o.reshape(B, dil, L, GROUP_WIDTH), lse.reshape(B, dil, L, GROUP_WIDTH)


def _dft_angle(r, c, n):
    return ((r * c) & (n - 1)).astype(F32) * (2.0 * math.pi / n)


def _dft_base_body(n, cos_ref, sin_ref):
    rows, ncols = cos_ref.shape
    ang = _dft_angle(lax.broadcasted_iota(jnp.int32, (rows, ncols), 0),
                     lax.broadcasted_iota(jnp.int32, (rows, ncols), 1), n)
    cos_ref[...] = jnp.cos(ang)
    sin_ref[...] = jnp.sin(ang)


def _dft_table_body(n, neg_sin, cb_ref, sb_ref, o_ref):
    rows, ncols = cb_ref.shape
    ang0 = _dft_angle(pl.program_id(0) * rows, lax.broadcasted_iota(jnp.int32, (1, ncols), 1), n)
    c0, s0 = jnp.cos(ang0), jnp.sin(ang0)
    cb, sb = cb_ref[...], sb_ref[...]
    sn = s0 * cb + c0 * sb
    o_ref[:, :ncols] = (c0 * cb - s0 * sb).astype(o_ref.dtype)
    o_ref[:, ncols:] = (-sn if neg_sin else sn).astype(o_ref.dtype)


def _dft_table(n, ncols, neg_sin, rows):
    assert n & (n - 1) == 0
    base_sds = jax.ShapeDtypeStruct((rows, ncols), F32)
    cb, sb = pl.pallas_call(
        functools.partial(_dft_base_body, n),
        out_shape=[base_sds, base_sds],
        compiler_params=_cparams(()),
        name=f"dft_base_{n}",
    )()
    return pl.pallas_call(
        functools.partial(_dft_table_body, n, neg_sin),
        out_shape=jax.ShapeDtypeStruct((n, 2 * ncols), BF16),
        grid=(n // rows,),
        in_specs=[pl.BlockSpec((rows, ncols), lambda i: (0, 0))] * 2,
        out_specs=pl.BlockSpec((rows, 2 * ncols), lambda i: (i, 0)),
        compiler_params=_cparams(("parallel",)),
        name=f"dft_table_{n}",
    )(cb, sb)


def _dft_tables(seq_lens):
    wc = _dft_table(FNET_GROUP_DIM, FNET_GROUP_DIM, False, FNET_GROUP_DIM)
    return wc, {s: _dft_table(s, s // 2, True, 256) for s in seq_lens}


FOLD_ROWS = 128


def _fourier_body(S, ua_ref, ub_ref, wc_ref, ws_ref, lo_ref, hi_ref, y_scr, ymid_scr, g_scr):
    C = FNET_GROUP_DIM
    H = S // 2
    R = FOLD_ROWS
    tr = lo_ref.shape[1]
    t = pl.program_id(1)
    i = pl.num_programs(1) - 1 - t
    scale = 1.0 / math.sqrt(S * C)
    flip = (lax.broadcasted_iota(jnp.int32, (R, 2 * R), 1)
            == R - lax.broadcasted_iota(jnp.int32, (R, 2 * R), 0)).astype(BF16)

    @pl.when(t == 0)
    def _():
        cosc, sinc = wc_ref[:, :C], wc_ref[:, C:]
        for u_ref, base in ((ua_ref, 0), (ub_ref, COL_BLK)):
            for b in range(H // R):
                if b == 0:
                    rev = jnp.dot(flip[:, :R], u_ref[0, S - R:S, :], preferred_element_type=F32)
                else:
                    rev = jnp.dot(flip, u_ref[0, S - (b + 1) * R:S - (b - 1) * R, :],
                                  preferred_element_type=F32)
                u = u_ref[0, b * R:(b + 1) * R, :].astype(F32)
                usum, udif = (u + rev).astype(BF16), (u - rev).astype(BF16)
                for k in range(COL_BLK // C):
                    c0 = base + k * C
                    y_scr[b * R:(b + 1) * R, c0:c0 + C] = jnp.dot(
                        usum[:, k * C:(k + 1) * C], cosc, preferred_element_type=F32).astype(BF16)
                    y_scr[H + b * R:H + (b + 1) * R, c0:c0 + C] = jnp.dot(
                        udif[:, k * C:(k + 1) * C], sinc, preferred_element_type=F32).astype(BF16)
            for k in range(COL_BLK // C):
                c0 = base + k * C
                ymid_scr[:, c0:c0 + C] = jnp.dot(u_ref[0, H:H + ROW_CHUNK, k * C:(k + 1) * C], cosc,
                                                 preferred_element_type=F32)
        col = lax.broadcasted_iota(jnp.int32, (ROW_CHUNK, H), 1)
        alt = (1 - 2 * (col & 1)).astype(BF16)
        p_mid = jnp.dot(alt, y_scr[:H, :], preferred_element_type=F32) + ymid_scr[...]
        g_scr[tr:tr + R, :] = jnp.zeros((R, g_scr.shape[1]), BF16)
        g_scr[tr:tr + ROW_CHUNK, :] = (p_mid * scale).astype(BF16)

    pos = i * tr + lax.broadcasted_iota(jnp.int32, (tr, 1), 0)
    sign = (1 - 2 * (pos & 1)).astype(F32)
    p = jnp.dot(ws_ref[:, :H], y_scr[:H, :], preferred_element_type=F32) + sign * ymid_scr[0:1, :]
    qn = jnp.dot(ws_ref[:, H:], y_scr[H:, :], preferred_element_type=F32)
    lo_ref[0] = ((p + qn) * scale).astype(lo_ref.dtype)
    g_scr[:tr, :] = ((p - qn) * scale).astype(BF16)
    for u in range(tr // R):
        hi_ref[0, u * R:(u + 1) * R, :] = jnp.dot(
            flip, g_scr[tr - (u + 1) * R:tr - (u - 1) * R, :], preferred_element_type=F32
        ).astype(hi_ref.dtype)
    g_scr[tr:tr + R, :] = g_scr[:R, :]


def _fourier(z3, wc, ws, *, tr=512):
    B, S, C = z3.shape
    H = S // 2
    nb = H // tr
    cb = NAT_FNET_COL // COL_BLK
    assert ws.shape == (S, S) and H % FOLD_ROWS == 0 and tr % FOLD_ROWS == 0
    half_sds = jax.ShapeDtypeStruct((B, H, FNET_WIDTH), BF16)
    return pl.pallas_call(
        functools.partial(_fourier_body, S),
        out_shape=[half_sds, half_sds],
        grid=(B, nb),
        in_specs=[
            pl.BlockSpec((1, S, COL_BLK), lambda b, t: (b, 0, cb)),
            pl.BlockSpec((1, S, COL_BLK), lambda b, t: (b, 0, cb + 1)),
            pl.BlockSpec((FNET_GROUP_DIM, 2 * FNET_GROUP_DIM), lambda b, t: (0, 0)),
            pl.BlockSpec((tr, S), lambda b, t: (nb - 1 - t, 0)),
        ],
        out_specs=[pl.BlockSpec((1, tr, FNET_WIDTH), lambda b, t: (b, nb - 1 - t, 0)),
                   pl.BlockSpec((1, tr, FNET_WIDTH), lambda b, t: (b, t, 0))],
        scratch_shapes=[pltpu.VMEM((S, FNET_WIDTH), BF16), pltpu.VMEM((ROW_CHUNK, FNET_WIDTH), F32),
                        pltpu.VMEM((tr + FOLD_ROWS, FNET_WIDTH), BF16)],
        compiler_params=_cparams(("parallel", "arbitrary")),
        name="fourier",
    )(z3, z3, wc, ws)


def _memory_body(qa_ref, qb_ref, mem_ref, gm_ref, wkv_ref, o_ref, k_scr, v_scr):
    E = MEM_HEAD_DIM

    @pl.when(pl.program_id(1) == 0)
    def _():
        hm = _rms_f32(mem_ref[0], gm_ref[...]).astype(BF16)
        kv = jnp.dot(hm, wkv_ref[...], preferred_element_type=F32)
        k_scr[...] = (kv[:, :MEM_WIDTH] * (E ** -0.5)).astype(BF16)
        v_scr[...] = kv[:, MEM_WIDTH:].astype(BF16)

    for q_ref, base in ((qa_ref, 0), (qb_ref, COL_BLK)):
        for k in range(COL_BLK // E):
            c0 = base + k * E
            s = lax.dot_general(q_ref[0, :, k * E:(k + 1) * E], k_scr[:, c0:c0 + E],
                                (((1,), (1,)), ((), ())), preferred_element_type=F32)
            m = jnp.max(s, axis=-1, keepdims=True)
            p = jnp.exp(s - m)
            den = jnp.sum(p, axis=-1, keepdims=True)
            o = jnp.dot(p.astype(BF16), v_scr[:, c0:c0 + E], preferred_element_type=F32) / den
            o_ref[0, :, c0:c0 + E] = o.astype(o_ref.dtype)


def _memory(z3, mem, g_mem, w_kv, *, tq=1024):
    B, S, C = z3.shape
    M, D = mem.shape[1:]
    cb = NAT_MEM_COL // COL_BLK
    assert MEM_HEAD_DIM ** -0.5 == 2.0 ** round(math.log2(MEM_HEAD_DIM ** -0.5))
    return pl.pallas_call(
        _memory_body,
        out_shape=jax.ShapeDtypeStruct((B, S, MEM_WIDTH), BF16),
        grid=(B, S // tq),
        in_specs=[
            pl.BlockSpec((1, tq, COL_BLK), lambda b, i: (b, i, cb)),
            pl.BlockSpec((1, tq, COL_BLK), lambda b, i: (b, i, cb + 1)),
            pl.BlockSpec((1, M, D), lambda b, i: (b, 0, 0)),
            pl.BlockSpec((1, D), lambda b, i: (0, 0)),
            pl.BlockSpec((D, 2 * MEM_WIDTH), lambda b, i: (0, 0)),
        ],
        out_specs=pl.BlockSpec((1, tq, MEM_WIDTH), lambda b, i: (b, i, 0)),
        scratch_shapes=[pltpu.VMEM((M, MEM_WIDTH), BF16), pltpu.VMEM((M, MEM_WIDTH), BF16)],
        compiler_params=_cparams(("parallel", "arbitrary")),
        name="memory_attn",
    )(z3, z3, mem, g_mem, w_kv).reshape(B * S, MEM_WIDTH)


def _combine_body(o0_ref, o1_ref, o2_ref, l0_ref, l1_ref, l2_ref, a_ref, nat_scr):
    tm = a_ref.shape[0]
    nc = GROUP_WIDTH // LANES

    def to_natural(ref, slot):
        dil = ref.shape[1]
        for r in range(dil):
            for c in range(nc):
                nat_scr[slot * nc + c, pl.ds(r, tm // dil, stride=dil), :] = (
                    ref[0, r, :, c * LANES:(c + 1) * LANES].astype(F32))

    def natural_rows(ref, slot, rows):
        if ref.shape[1] == 1:
            return ref[0, 0, rows, :].astype(F32)
        return jnp.concatenate([nat_scr[slot * nc + c, rows, :] for c in range(nc)], axis=1)

    slots = {}
    for ref in (l1_ref, l2_ref, o1_ref, o2_ref):
        slots[id(ref)] = len(slots)
        to_natural(ref, slots[id(ref)])

    def combine(rows):
        l0 = natural_rows(l0_ref, None, rows)
        l1 = natural_rows(l1_ref, slots[id(l1_ref)], rows)
        l2 = natural_rows(l2_ref, slots[id(l2_ref)], rows)
        mx = jnp.maximum(jnp.maximum(l0, l1), l2)
        e0, e1, e2 = jnp.exp(l0 - mx), jnp.exp(l1 - mx), jnp.exp(l2 - mx)
        num = (e0 * natural_rows(o0_ref, None, rows)
               + e1 * natural_rows(o1_ref, slots[id(o1_ref)], rows)
               + e2 * natural_rows(o2_ref, slots[id(o2_ref)], rows))
        a_ref[rows, :] = (num / (e0 + e1 + e2)).astype(BF16)

    _for_row_chunks(tm, combine)


def _combine(seq, o_groups, lse_groups, *, tm=512):
    B = o_groups[0].shape[0]
    tps = seq // tm
    group_specs = [pl.BlockSpec((1, a.shape[1], tm // a.shape[1], GROUP_WIDTH),
                                lambda i: (i // tps, 0, i % tps, 0))
                   for a in (*o_groups, *lse_groups)]
    return pl.pallas_call(
        _combine_body,
        out_shape=jax.ShapeDtypeStruct((B * seq, GROUP_WIDTH), BF16),
        grid=(B * tps,),
        in_specs=group_specs,
        out_specs=pl.BlockSpec((tm, GROUP_WIDTH), lambda i: (i, 0)),
        scratch_shapes=[pltpu.VMEM((2 * (len(DILATED_GROUPS) - 1) * GROUP_WIDTH // LANES, tm, LANES), F32)],
        compiler_params=_cparams(("parallel",)),
        name="group_combine",
    )(*o_groups, *lse_groups)


def _gated_body(tiles_per_half, h_ref, a_ref, flo_ref, fhi_ref, m_ref, wg0_ref, wg1_ref, wg2_ref,
                bg0_ref, bg1_ref, bg2_ref, wpa_ref, wpf_ref, wpm_ref, out_ref):
    h = h_ref[...]
    in_first_half = (pl.program_id(0) // tiles_per_half) % 2 == 0
    f = jnp.where(in_first_half, flo_ref[...], fhi_ref[...])

    def mm(x, w_ref):
        return jnp.dot(x, w_ref[...], preferred_element_type=F32)

    def gate(w_ref, b_ref):
        return jax.nn.sigmoid(mm(h, w_ref) + b_ref[...])

    mg = gate(wg0_ref, bg0_ref) * mm(a_ref[...], wpa_ref)
    mg += gate(wg1_ref, bg1_ref) * mm(f, wpf_ref)
    mg += gate(wg2_ref, bg2_ref) * mm(m_ref[...], wpm_ref)
    out_ref[...] = mg.astype(out_ref.dtype)


def _gated_merge(h, a, f_halves, m, w_gate, b_gate, w_pa, w_pf, w_pm, *, tm=1024, tn=512):
    T, D = h.shape
    nj = D // tn
    B, H, _ = f_halves[0].shape
    tph = H // tm
    assert tph >= 1 and T == 2 * B * H
    f_lo, f_hi = (x.reshape(B * H, FNET_WIDTH) for x in f_halves)
    row = lambda i, j: (i, 0)
    colj = lambda i, j: (0, j)
    half_row = lambda i, j: ((i // (2 * tph)) * tph + i % tph, 0)
    gate_specs = [pl.BlockSpec((D, tn), lambda i, j, k=k: (0, j + k * nj)) for k in range(3)]
    bias_specs = [pl.BlockSpec((1, tn), lambda i, j, k=k: (0, j + k * nj)) for k in range(3)]
    return pl.pallas_call(
        functools.partial(_gated_body, tph),
        out_shape=jax.ShapeDtypeStruct((T, D), BF16),
        grid=(T // tm, nj),
        in_specs=[pl.BlockSpec((tm, D), row), pl.BlockSpec((tm, GROUP_WIDTH), row),
                  pl.BlockSpec((tm, FNET_WIDTH), half_row), pl.BlockSpec((tm, FNET_WIDTH), half_row),
                  pl.BlockSpec((tm, MEM_WIDTH), row)]
                 + gate_specs + bias_specs
                 + [pl.BlockSpec((GROUP_WIDTH, tn), colj), pl.BlockSpec((FNET_WIDTH, tn), colj),
                    pl.BlockSpec((MEM_WIDTH, tn), colj)],
        out_specs=pl.BlockSpec((tm, tn), lambda i, j: (i, j)),
        compiler_params=_cparams(("parallel", "parallel")),
        name="gated_merge",
    )(h, a, f_lo, f_hi, m, w_gate, w_gate, w_gate, b_gate, b_gate, b_gate, w_pa, w_pf, w_pm)


def _out_proj_body(mg_ref, x_ref, wo_ref, gpost_ref, out_ref, y_scr):
    y_scr[...] = jnp.dot(mg_ref[...], wo_ref[...], preferred_element_type=F32)
    gpost = gpost_ref[...]

    def post(rows):
        out_ref[rows, :] = x_ref[rows, :] + _rms_f32(y_scr[rows, :], gpost)

    _for_row_chunks(x_ref.shape[0], post)


def _out_proj(mg, x, w_out, g_post, *, tm=512):
    T, D = x.shape
    row = lambda i: (i, 0)
    const = lambda i: (0, 0)
    return pl.pallas_call(
        _out_proj_body,
        out_shape=jax.ShapeDtypeStruct((T, D), F32),
        grid=(T // tm,),
        in_specs=[pl.BlockSpec((tm, D), row), pl.BlockSpec((tm, D), row),
                  pl.BlockSpec((D, D), const), pl.BlockSpec((1, D), const)],
        out_specs=pl.BlockSpec((tm, D), row),
        scratch_shapes=[pltpu.VMEM((tm, D), F32)],
        compiler_params=_cparams(("parallel",)),
        name="out_proj",
    )(mg, x, w_out, g_post)


def _layer_params(l, ffn1_norm_pre, ffn1_w_in, ffn1_w_out, ffn1_norm_post, mix_norm_pre, mem_norm, w_in,
                  w_mem_kv, w_gate, b_gate, w_proj_attn, w_proj_fnet, w_proj_mem, w_out, mix_norm_post,
                  ffn2_norm_pre, ffn2_w_in, ffn2_w_out, ffn2_norm_post):
    vec = lambda a: a[l][None, :]
    mat = lambda a: a[l].astype(BF16)
    wi = mat(w_in)
    qkv = lambda g: [wi[:, s * ATTN_WIDTH + g * GROUP_WIDTH: s * ATTN_WIDTH + (g + 1) * GROUP_WIDTH]
                     for s in range(3)]
    w_nat = jnp.concatenate(qkv(0) + [wi[:, 3 * ATTN_WIDTH:]], axis=1)
    w_groups = [jnp.concatenate(qkv(g), axis=1) for g in range(1, len(DILATED_GROUPS))]
    return dict(f1_pre=vec(ffn1_norm_pre), f1_wi=mat(ffn1_w_in), f1_wo=mat(ffn1_w_out),
                f1_post=vec(ffn1_norm_post), mix_pre=vec(mix_norm_pre), mem_norm=vec(mem_norm),
                w_nat=w_nat, w_groups=w_groups, w_mem_kv=mat(w_mem_kv), w_gate=mat(w_gate),
                b_gate=vec(b_gate), w_pa=mat(w_proj_attn), w_pf=mat(w_proj_fnet),
                w_pm=mat(w_proj_mem), w_out=mat(w_out), mix_post=vec(mix_norm_post),
                f2_pre=vec(ffn2_norm_pre), f2_wi=mat(ffn2_w_in), f2_wo=mat(ffn2_w_out),
                f2_post=vec(ffn2_norm_post))


def _layer(x, mem, bias, wc, ws, p):
    B, S, D = x.shape
    dils = [dil for _, dil in DILATED_GROUPS]
    assert dils[0] == 1
    xt = x.reshape(B * S, D)
    x1 = _ffn(xt, p["f1_pre"], p["f1_wi"], p["f1_wo"], p["f1_post"])
    hs = _norm_regroup(x1, S, p["mix_pre"], dils)
    h = hs[0].reshape(B * S, D)
    z0 = _in_proj(h, p["w_nat"])
    zs = [z0] + [_in_proj(hd.reshape(B * S, D), w) for hd, w in zip(hs[1:], p["w_groups"])]
    o_groups, lse_groups = [], []
    for g, (dil, z) in enumerate(zip(dils, zs)):
        o, lse = _dilated_group(z.reshape(B, dil, S // dil, z.shape[-1]), bias, g)
        o_groups.append(o)
        lse_groups.append(lse)
    z3 = z0.reshape(B, S, -1)
    f = _fourier(z3, wc, ws, tr=min(512, S // 4))
    m = _memory(z3, mem, p["mem_norm"], p["w_mem_kv"])
    a = _combine(S, o_groups, lse_groups)
    mg = _gated_merge(h, a, f, m, p["w_gate"], p["b_gate"], p["w_pa"], p["w_pf"], p["w_pm"])
    x2 = _out_proj(mg, x1, p["w_out"], p["mix_post"])
    x3 = _ffn(x2, p["f2_pre"], p["f2_wi"], p["f2_wo"], p["f2_post"])
    return x3.reshape(B, S, D)


def kernel(x_prompt, x_sample, mem_prompt, mem_sample, rel_bias, ffn1_norm_pre, ffn1_w_in, ffn1_w_out, ffn1_norm_post, mix_norm_pre, mem_norm, w_in, w_mem_kv, w_gate, b_gate, w_proj_attn, w_proj_fnet, w_proj_mem, w_out, mix_norm_post, ffn2_norm_pre, ffn2_w_in, ffn2_w_out, ffn2_norm_post):
    depth = w_in.shape[0]
    bias = _bias_tables(rel_bias)
    wc, ws = _dft_tables({x_prompt.shape[1], x_sample.shape[1]})
    y_prompt, y_sample = x_prompt, x_sample
    for l in range(depth):
        p = _layer_params(l, ffn1_norm_pre, ffn1_w_in, ffn1_w_out, ffn1_norm_post, mix_norm_pre, mem_norm,
                          w_in, w_mem_kv, w_gate, b_gate, w_proj_attn, w_proj_fnet, w_proj_mem, w_out,
                          mix_norm_post, ffn2_norm_pre, ffn2_w_in, ffn2_w_out, ffn2_norm_post)
        y_prompt = _layer(y_prompt, mem_prompt, bias, wc, ws[y_prompt.shape[1]], p)
        y_sample = _layer(y_sample, mem_sample, bias, wc, ws[y_sample.shape[1]], p)
    return (y_prompt, y_sample)
```
